```python
import jax, jax.numpy as jnp
from jax import lax
import numpy as np

D_MODEL = 1024
BATCH = 1
SEQ = 16384
DEPTH = 1
DEC_BATCH = 16
DEC_SEQ = 4096
PAST_LEN = 128

M_HEADS = 4
M_HEAD_DIM = 128
M_WIDTH = M_HEADS * M_HEAD_DIM
M_CHUNK = 128
A_HEADS = 8
A_KV_HEADS = 4
A_HEAD_DIM = 64
A_WIDTH = A_HEADS * A_HEAD_DIM
A_KV_WIDTH = A_KV_HEADS * A_HEAD_DIM
WINDOW = 128
A_BLOCK = 128
ROT_DIM = A_HEAD_DIM // 4
ROPE_THETA = 500000.0
D_FF = 2816
CONV_WIDTH = 3
EPS = 1e-6
D_IN = 4 * M_WIDTH + 4 * M_HEADS + A_WIDTH + 2 * A_KV_WIDTH + 2 * D_MODEL

kernel_name = "bidir_mlstm_swa_hybrid_encoder"


def _split_points():
    sizes = [M_WIDTH] * 4 + [2 * M_HEADS] * 2 + [A_WIDTH, A_KV_WIDTH, A_KV_WIDTH, D_MODEL, D_MODEL]
    return np.cumsum(sizes)[:-1].tolist()


def rmsnorm(x, w):
    xf = x.astype(jnp.float32)
    y = xf * lax.rsqrt(jnp.mean(xf * xf, axis=-1, keepdims=True) + EPS)
    return (y * w.astype(jnp.float32)).astype(x.dtype)


def mlstm_chunkwise(q, k, v, i_pre, f_pre):
    B, H, S, dh = q.shape
    L = M_CHUNK
    N = S // L
    q = q.reshape(B, H, N, L, dh)
    k = k.reshape(B, H, N, L, dh)
    v = v.reshape(B, H, N, L, dh)
    ig = i_pre.reshape(B, H, N, L)
    b = jnp.cumsum(jax.nn.log_sigmoid(f_pre).reshape(B, H, N, L), axis=-1)
    b_tot = b[..., -1]
    a = b_tot[..., None] - b + ig
    m_loc = jnp.max(a, axis=-1)
    wgt = jnp.exp(a - m_loc[..., None])
    C_loc = jnp.einsum('bhnlk,bhnlv->bhnkv', wgt[..., None] * k, v)
    n_loc = jnp.einsum('bhnl,bhnlk->bhnk', wgt, k)

    def step(carry, inp):
        C, n, m = carry
        bt, ml, Cl, nl = inp
        m_new = jnp.maximum(bt + m, ml)
        s_prev = jnp.exp(bt + m - m_new)
        s_loc = jnp.exp(ml - m_new)
        C_new = s_prev[..., None, None] * C + s_loc[..., None, None] * Cl
        n_new = s_prev[..., None] * n + s_loc[..., None] * nl
        return (C_new, n_new, m_new), (C, n, m)

    init = (jnp.zeros((B, H, dh, dh), jnp.float32), jnp.zeros((B, H, dh), jnp.float32),
            jnp.zeros((B, H), jnp.float32))
    xs = (jnp.moveaxis(b_tot, 2, 0), jnp.moveaxis(m_loc, 2, 0),
          jnp.moveaxis(C_loc, 2, 0), jnp.moveaxis(n_loc, 2, 0))
    _, (C_prev, n_prev, m_prev) = lax.scan(step, init, xs)
    C_prev = jnp.moveaxis(C_prev, 0, 2)
    n_prev = jnp.moveaxis(n_prev, 0, 2)
    m_prev = jnp.moveaxis(m_prev, 0, 2)

    D = b[..., :, None] - b[..., None, :] + ig[..., None, :]
    tril = jnp.tril(jnp.ones((L, L), dtype=bool))
    D = jnp.where(tril, D, -jnp.inf)
    inter_log = b + m_prev[..., None]
    m_t = jnp.maximum(inter_log, jnp.max(D, axis=-1))
    Dw = jnp.exp(D - m_t[..., None])
    inter_w = jnp.exp(inter_log - m_t)
    qk = jnp.einsum('bhntd,bhnsd->bhnts', q, k) * Dw
    num = jnp.einsum('bhnts,bhnsd->bhntd', qk, v) + inter_w[..., None] * jnp.einsum('bhntk,bhnkv->bhntv', q, C_prev)
    den = jnp.sum(qk, axis=-1) + inter_w * jnp.einsum('bhntk,bhnk->bhnt', q, n_prev)
    h = num / jnp.maximum(jnp.abs(den), jnp.exp(-m_t))[..., None]
    return h.reshape(B, H, S, dh)


def rope_partial(x, pos):
    half = ROT_DIM // 2
    inv = ROPE_THETA ** (-jnp.arange(half, dtype=jnp.float32) / half)
    ang = pos.astype(jnp.float32)[:, None] * inv[None, :]
    cos, sin = jnp.cos(ang), jnp.sin(ang)
    xr = x[..., :ROT_DIM].astype(jnp.float32)
    x1, x2 = xr[..., :half], xr[..., half:]
    rot = jnp.concatenate([x1 * cos - x2 * sin, x2 * cos + x1 * sin], axis=-1)
    return jnp.concatenate([rot.astype(x.dtype), x[..., ROT_DIM:]], axis=-1)


def window_attention(q, k, v, sinks):
    B, Hq, S, dh = q.shape
    Hkv = k.shape[1]
    G = Hq // Hkv
    L = A_BLOCK
    N = S // L
    qb = q.reshape(B, Hkv, G, N, L, dh)

    def band(t):
        tp = jnp.pad(t, ((0, 0), (0, 0), (L, L), (0, 0))).reshape(B, Hkv, N + 2, L, dh)
        return jnp.concatenate([tp[:, :, :-2], tp[:, :, 1:-1], tp[:, :, 2:]], axis=3)

    kb, vb = band(k), band(v)
    s = jnp.einsum('bhgnqd,bhnkd->bhgnqk', qb, kb).astype(jnp.float32) * (dh ** -0.5)
    qi = jnp.arange(L)
    kj = jnp.arange(3 * L)
    rel = kj[None, :] - L - qi[:, None]
    key_pos = jnp.arange(N)[:, None] * L - L + kj[None, :]
    mask = (jnp.abs(rel) <= WINDOW)[None, :, :] & ((key_pos >= 0) & (key_pos < S))[:, None, :]
    s = jnp.where(mask, s, -jnp.inf)
    sink = sinks.astype(jnp.float32).reshape(Hkv, G)[None, :, :, None, None, None]
    m = jnp.maximum(jnp.max(s, axis=-1, keepdims=True), sink)
    p = jnp.exp(s - m)
    p = p / (jnp.sum(p, axis=-1, keepdims=True) + jnp.exp(sink - m))
    o = jnp.einsum('bhgnqk,bhnkd->bhgnqd', p.astype(v.dtype), vb)
    return o.reshape(B, Hq, S, dh)


def hybrid_mixer(xn, w_in, i_bias, f_bias, mh_norm_w, attn_sink, w_proj_m, w_proj_a, w_out):
    B, S, _ = xn.shape
    proj = xn @ w_in
    mq, mk, mv, mo, mig, mfg, aq, ak, av, gm, ga = jnp.split(proj, _split_points(), axis=-1)

    def heads(t, H):
        return t.reshape(B, S, H, -1).transpose(0, 2, 1, 3)

    f32 = jnp.float32
    q = heads(mq, M_HEADS).astype(f32)
    k = heads(mk, M_HEADS).astype(f32) * (M_HEAD_DIM ** -0.5)
    v = heads(mv, M_HEADS).astype(f32)
    ig = (mig.astype(f32).reshape(B, S, 2, M_HEADS) + i_bias.astype(f32)).transpose(2, 0, 3, 1)
    fg = (mfg.astype(f32).reshape(B, S, 2, M_HEADS) + f_bias.astype(f32)).transpose(2, 0, 3, 1)
    flip = lambda t: jnp.flip(t, axis=2)
    h_fwd = mlstm_chunkwise(q, k, v, ig[0], fg[0])
    h_bwd = flip(mlstm_chunkwise(flip(q), flip(k), flip(v), flip(ig[1]), flip(fg[1])))
    h = h_fwd + h_bwd
    h = h - jnp.mean(h, axis=-1, keepdims=True)
    h = h * lax.rsqrt(jnp.mean(h * h, axis=-1, keepdims=True) + EPS)
    h = h.transpose(0, 2, 1, 3).reshape(B, S, M_WIDTH) * mh_norm_w.astype(f32)
    h = (jax.nn.sigmoid(mo.astype(f32)) * h).astype(xn.dtype)
    y_m = h @ w_proj_m

    pos = jnp.arange(S)
    qa = rope_partial(heads(aq, A_HEADS), pos)
    ka = rope_partial(heads(ak, A_KV_HEADS), pos)
    va = heads(av, A_KV_HEADS)
    oa = window_attention(qa, ka, va, attn_sink)
    y_a = oa.transpose(0, 2, 1, 3).reshape(B, S, A_WIDTH) @ w_proj_a

    y = jax.nn.sigmoid(gm) * y_m + jax.nn.sigmoid(ga) * y_a
    return y @ w_out


def conv_ffn(xn, w_up, conv_w, conv_b, w_down):
    S = xn.shape[1]
    u = xn @ w_up
    pad = CONV_WIDTH // 2
    up = jnp.pad(u, ((0, 0), (pad, pad), (0, 0)))
    u = sum(up[:, j:j + S] * conv_w[j] for j in range(CONV_WIDTH)) + conv_b
    a, b = jnp.split(u, 2, axis=-1)
    return (jax.nn.silu(a) * b) @ w_down


def encoder(x, norm1_w, w_in, i_bias, f_bias, mh_norm_w, attn_sink, w_proj_m, w_proj_a, w_out,
            norm2_w, w_up, conv_w, conv_b, w_down, norm_f_w):
    for l in range(DEPTH):
        x = x + hybrid_mixer(rmsnorm(x, norm1_w[l]), w_in[l], i_bias[l], f_bias[l], mh_norm_w[l],
                             attn_sink[l], w_proj_m[l], w_proj_a[l], w_out[l])
        x = x + conv_ffn(rmsnorm(x, norm2_w[l]), w_up[l], conv_w[l], conv_b[l], w_down[l])
    return rmsnorm(x, norm_f_w)


def setup_inputs(seed: int = 0) -> dict:
    key = jax.random.key(seed)
    ks = jax.random.split(key, 18)
    f32 = jnp.float32
    nrm = lambda k, shape: jax.random.normal(k, shape, f32)
    lin = lambda k, shape, fan_in: nrm(k, shape) * (fan_in ** -0.5)
    return {
        "x_prompt": nrm(ks[0], (BATCH, SEQ, D_MODEL)),
        "x_sample": nrm(ks[1], (DEC_BATCH, DEC_SEQ, D_MODEL)),
        "norm1_w": 1.0 + 0.02 * nrm(ks[2], (DEPTH, D_MODEL)),
        "w_in": lin(ks[3], (DEPTH, D_MODEL, D_IN), D_MODEL),
        "i_bias": 0.1 * nrm(ks[4], (DEPTH, 2, M_HEADS)),
        "f_bias": jax.random.uniform(ks[5], (DEPTH, 2, M_HEADS), f32, 3.0, 6.0),
        "mh_norm_w": 1.0 + 0.02 * nrm(ks[6], (DEPTH, M_WIDTH)),
        "attn_sink": 0.5 * nrm(ks[7], (DEPTH, A_HEADS)),
        "w_proj_m": lin(ks[8], (DEPTH, M_WIDTH, D_MODEL), M_WIDTH),
        "w_proj_a": lin(ks[9], (DEPTH, A_WIDTH, D_MODEL), A_WIDTH),
        "w_out": lin(ks[10], (DEPTH, D_MODEL, D_MODEL), D_MODEL),
        "norm2_w": 1.0 + 0.02 * nrm(ks[11], (DEPTH, D_MODEL)),
        "w_up": lin(ks[12], (DEPTH, D_MODEL, 2 * D_FF), D_MODEL),
        "conv_w": lin(ks[13], (DEPTH, CONV_WIDTH, 2 * D_FF), CONV_WIDTH),
        "conv_b": 0.02 * nrm(ks[14], (DEPTH, 2 * D_FF)),
        "w_down": lin(ks[15], (DEPTH, D_FF, D_MODEL), D_FF),
        "norm_f_w": 1.0 + 0.02 * nrm(ks[16], (D_MODEL,)),
    }


def reference(x_prompt, x_sample, norm1_w, w_in, i_bias, f_bias, mh_norm_w, attn_sink, w_proj_m, w_proj_a,
              w_out, norm2_w, w_up, conv_w, conv_b, w_down, norm_f_w):
    y_prompt = encoder(x_prompt, norm1_w, w_in, i_bias, f_bias, mh_norm_w, attn_sink, w_proj_m, w_proj_a,
                       w_out, norm2_w, w_up, conv_w, conv_b, w_down, norm_f_w)
    y_sample = encoder(x_sample, norm1_w, w_in, i_bias, f_bias, mh_norm_w, attn_sink, w_proj_m, w_proj_a,
                       w_out, norm2_w, w_up, conv_w, conv_b, w_down, norm_f_w)
    return (y_prompt, y_sample)
```

```python
import functools

import jax
import jax.numpy as jnp
from jax import lax
from jax.experimental import pallas as pl
from jax.experimental.pallas import tpu as pltpu

F32 = jnp.float32
BF16 = jnp.bfloat16

D_MODEL = 1024
M_HEADS = 4
M_HEAD_DIM = 128
M_WIDTH = M_HEADS * M_HEAD_DIM
M_CHUNK = 128
A_HEADS = 8
A_KV_HEADS = 4
A_HEAD_DIM = 64
A_WIDTH = A_HEADS * A_HEAD_DIM
A_KV_WIDTH = A_KV_HEADS * A_HEAD_DIM
WINDOW = 128
A_BLOCK = 128
ROT_DIM = A_HEAD_DIM // 4
ROPE_THETA = 500000.0
D_FF = 2816
EPS = 1e-6

LANES = 128
TOKEN_TILE = 512
FFN_HALO = 16
FFN_CHUNK = 256
VMEM_LIMIT = 56 * 1024 * 1024


def _resident(shape):
    return pl.BlockSpec(shape, lambda *_: (0,) * len(shape), pipeline_mode=pl.Buffered(1))


def _rmsnorm(x, w):
    return x * lax.rsqrt(jnp.mean(x * x, axis=-1, keepdims=True) + EPS) * w


def _sigmoid(x):
    return 1.0 / (1.0 + jnp.exp(-x))


def _in_proj_kernel(x_ref, nw_ref, wm_ref, wg_ref, wa_ref, gb_ref, cos_ref, sa_ref, sb_ref,
                    m_out, g_out, a_out):
    xb = _rmsnorm(x_ref[...], nw_ref[...]).astype(BF16)
    m_out[...] = jnp.dot(xb, wm_ref[...], preferred_element_type=F32).astype(BF16)
    g_out[...] = jnp.dot(xb, wg_ref[...], preferred_element_type=F32) + gb_ref[...]
    a = jnp.dot(xb, wa_ref[...], preferred_element_type=F32)
    cos, sa, sb = cos_ref[...], sa_ref[...], sb_ref[...]
    half = ROT_DIM // 2
    lane = lax.broadcasted_iota(jnp.int32, (1, LANES), 1)
    low = lane < A_HEAD_DIM

    def rope(t):
        return t * cos + pltpu.roll(t, LANES - half, 1) * sa + pltpu.roll(t, half, 1) * sb

    def dup_heads(t):
        sw = pltpu.roll(t, A_HEAD_DIM, 1)
        return jnp.where(low, t, sw), jnp.where(low, sw, t)

    for j in range(A_WIDTH // LANES):
        a_out[:, j * LANES:(j + 1) * LANES] = rope(a[:, j * LANES:(j + 1) * LANES]).astype(BF16)
    k_off, v_off = A_WIDTH, A_WIDTH + A_KV_WIDTH
    for j in range(A_KV_WIDTH // LANES):
        k0, k1 = dup_heads(rope(a[:, k_off + j * LANES:k_off + (j + 1) * LANES]))
        v0, v1 = dup_heads(a[:, v_off + j * LANES:v_off + (j + 1) * LANES])
        ko = A_WIDTH + 2 * j * LANES
        vo = 2 * A_WIDTH + 2 * j * LANES
        a_out[:, ko:ko + LANES] = k0.astype(BF16)
        a_out[:, ko + LANES:ko + 2 * LANES] = k1.astype(BF16)
        a_out[:, vo:vo + LANES] = v0.astype(BF16)
        a_out[:, vo + LANES:vo + 2 * LANES] = v1.astype(BF16)


def _in_proj(x2, seq_len, nw, wm, wg, wa, gb, cos_t, sa_t, sb_t):
    n_tok = x2.shape[0]
    T = TOKEN_TILE
    tiles_per_seq = seq_len // T
    full = _resident
    rope_spec = pl.BlockSpec((T, LANES), lambda i: (i % tiles_per_seq, 0))
    return pl.pallas_call(
        _in_proj_kernel,
        grid=(n_tok // T,),
        in_specs=[
            pl.BlockSpec((T, D_MODEL), lambda i: (i, 0)),
            full((1, D_MODEL)),
            full(wm.shape), full(wg.shape), full(wa.shape), full((1, LANES)),
            rope_spec, rope_spec, rope_spec,
        ],
        out_specs=[
            pl.BlockSpec((T, 4 * M_WIDTH), lambda i: (i, 0)),
            pl.BlockSpec((T, LANES), lambda i: (i, 0)),
            pl.BlockSpec((T, 3 * A_WIDTH), lambda i: (i, 0)),
        ],
        out_shape=[
            jax.ShapeDtypeStruct((n_tok, 4 * M_WIDTH), BF16),
            jax.ShapeDtypeStruct((n_tok, LANES), F32),
            jax.ShapeDtypeStruct((n_tok, 3 * A_WIDTH), BF16),
        ],
        compiler_params=pltpu.CompilerParams(
            dimension_semantics=("arbitrary",), vmem_limit_bytes=VMEM_LIMIT),
        name="in_proj",
    )(x2, nw, wm, wg, wa, gb, cos_t, sa_t, sb_t)


def _log_sigmoid(x):
    return jnp.minimum(x, 0.0) - jnp.log(1.0 + jnp.exp(-jnp.abs(x)))


def _cumsum_lanes(x):
    lane = lax.broadcasted_iota(jnp.int32, x.shape, 1)
    sh = 1
    while sh < x.shape[1]:
        x = x + jnp.where(lane >= sh, pltpu.roll(x, sh, 1), 0.0)
        sh *= 2
    return x


def _mlstm_kernel(qf_ref, kf_ref, vf_ref, gf_ref, qb_ref, kb_ref, vb_ref, gb_ref,
                  hf_out, hb_out, cn_ref, m_ref):
    L, DH, H = M_CHUNK, M_HEAD_DIM, M_HEADS
    scale = DH ** -0.5

    @pl.when(pl.program_id(1) == 0)
    def _():
        cn_ref[...] = jnp.zeros_like(cn_ref)
        m_ref[...] = jnp.zeros_like(m_ref)

    row = lax.broadcasted_iota(jnp.int32, (L, L), 0)
    col = lax.broadcasted_iota(jnp.int32, (L, L), 1)
    lane = lax.broadcasted_iota(jnp.int32, (L, DH), 1)
    ones_col = jnp.where(lane == 0, 1.0, 0.0).astype(BF16)

    dirs = ((qf_ref, kf_ref, vf_ref, gf_ref, hf_out), (qb_ref, kb_ref, vb_ref, gb_ref, hb_out))
    for d, (q_ref, k_ref, v_ref, g_ref, h_out) in enumerate(dirs):
        gt = g_ref[...].T
        i_rows = gt[0:2 * H]
        lf_rows = _log_sigmoid(gt[2 * H:4 * H])
        cum = _cumsum_lanes(lf_rows)
        tot = cum[:, L - 1:L]
        b_rows = cum if d == 0 else tot - cum + lf_rows
        a_rows = tot - b_rows + i_rows
        m_loc_all = jnp.max(a_rows, axis=1, keepdims=True)
        packed = jnp.concatenate(
            [b_rows, i_rows, jnp.zeros((L - 4 * H, L), F32)], axis=0)
        cols = packed.T
        causal = (col <= row) if d == 0 else (col >= row)

        for h in range(H):
            r = d * H + h
            sl = slice(h * DH, (h + 1) * DH)
            q = q_ref[:, sl]
            k = k_ref[:, sl]
            v = v_ref[:, sl]
            b_row, i_row = b_rows[r:r + 1], i_rows[r:r + 1]
            b_col, i_col = cols[:, r:r + 1], cols[:, 2 * H + r:2 * H + r + 1]
            b_tot = tot[r:r + 1]
            m_loc = m_loc_all[r:r + 1]
            m_prev = m_ref[r:r + 1, :]
            cn_prev = cn_ref[r]

            inter_log = b_col + m_prev[:, 0:1]
            dmat = jnp.where(causal, b_col - b_row + i_row, -jnp.inf)
            m_t = jnp.maximum(inter_log, jnp.max(dmat, axis=1, keepdims=True))
            dw = jnp.exp(dmat - m_t)
            inter_w = jnp.exp(inter_log - m_t)
            s = lax.dot_general(q, k, (((1,), (1,)), ((), ())), preferred_element_type=F32)
            p = (s * scale * dw).astype(BF16)
            qi = (q.astype(F32) * inter_w).astype(BF16)
            v_aug = jnp.concatenate([v, ones_col], axis=1)
            lhs = jnp.concatenate([p, qi], axis=1)
            rhs = jnp.concatenate([v_aug, cn_prev.astype(BF16)], axis=0)
            out = jnp.dot(lhs, rhs, preferred_element_type=F32)
            num = out[:, :DH]
            den = out[:, DH:DH + 1]
            h_out[:, sl] = num / jnp.maximum(jnp.abs(den), jnp.exp(-m_t))

            a_col = b_tot - b_col + i_col
            wgt = jnp.exp(a_col - m_loc) * scale
            kw = (k.astype(F32) * wgt).astype(BF16)
            cn_loc = lax.dot_general(kw, v_aug, (((0,), (0,)), ((), ())),
                                     preferred_element_type=F32)
            m_new = jnp.maximum(b_tot + m_prev, m_loc)
            s_prev = jnp.exp(b_tot + m_prev - m_new)
            s_loc = jnp.exp(m_loc - m_new)
            cn_ref[r] = s_prev[:, 0:1] * cn_prev + s_loc[:, 0:1] * cn_loc
            m_ref[r:r + 1, :] = m_new


def _mlstm(mqkvo, gates, n_seq, seq_len):
    n_tok = mqkvo.shape[0]
    L = M_CHUNK
    n_chunks = seq_len // L
    fwd = lambda j: (lambda s, c: (s * n_chunks + c, j))
    bwd = lambda j: (lambda s, c: (s * n_chunks + n_chunks - 1 - c, j))
    blk = lambda im: pl.BlockSpec((L, M_WIDTH), im)
    gblk = lambda im: pl.BlockSpec((L, LANES), im)
    return pl.pallas_call(
        _mlstm_kernel,
        grid=(n_seq, n_chunks),
        in_specs=[blk(fwd(0)), blk(fwd(1)), blk(fwd(2)), gblk(fwd(0)),
                  blk(bwd(0)), blk(bwd(1)), blk(bwd(2)), gblk(bwd(0))],
        out_specs=[blk(fwd(0)), blk(bwd(0))],
        out_shape=[jax.ShapeDtypeStruct((n_tok, M_WIDTH), F32)] * 2,
        scratch_shapes=[
            pltpu.VMEM((2 * M_HEADS, M_HEAD_DIM, 2 * M_HEAD_DIM), F32),
            pltpu.VMEM((2 * M_HEADS, LANES), F32),
        ],
        compiler_params=pltpu.CompilerParams(
            dimension_semantics=("arbitrary", "arbitrary"), vmem_limit_bytes=VMEM_LIMIT),
        name="mlstm",
    )(mqkvo, mqkvo, mqkvo, gates, mqkvo, mqkvo, mqkvo, gates)


def _mixer_out_kernel(x_ref, nw_ref, hf_ref, hb_ref, mo_ref, mhw_ref, q_ref,
                      kp_ref, kc_ref, kn_ref, vp_ref, vc_ref, vn_ref, sink_ref,
                      wg_ref, wpm_ref, wpa_ref, wo_ref, out_ref, *, tiles_per_seq):
    T, L = TOKEN_TILE, A_BLOCK
    t_in_seq = pl.program_id(0) % tiles_per_seq
    x = x_ref[...]
    xb = _rmsnorm(x, nw_ref[...]).astype(BF16)

    hsum = hf_ref[...] + hb_ref[...]
    parts = []
    for h in range(M_HEADS):
        hh = hsum[:, h * M_HEAD_DIM:(h + 1) * M_HEAD_DIM]
        hh = hh - jnp.mean(hh, axis=-1, keepdims=True)
        parts.append(hh * lax.rsqrt(jnp.mean(hh * hh, axis=-1, keepdims=True) + EPS))
    hn = jnp.concatenate(parts, axis=1) * mhw_ref[...]
    hg = (_sigmoid(mo_ref[...].astype(F32)) * hn).astype(BF16)
    y_m = jnp.dot(hg, wpm_ref[...], preferred_element_type=F32)

    k_ext = jnp.concatenate([kp_ref[...], kc_ref[...], kn_ref[...]], axis=0)
    v_ext = jnp.concatenate([vp_ref[...], vc_ref[...], vn_ref[...]], axis=0)
    lane = lax.broadcasted_iota(jnp.int32, (1, LANES), 1)
    low = lane < A_HEAD_DIM
    qi = lax.broadcasted_iota(jnp.int32, (L, 3 * L), 0)
    kj = lax.broadcasted_iota(jnp.int32, (L, 3 * L), 1)
    band = jnp.abs(kj - L - qi) <= WINDOW
    first_lo = jnp.where(t_in_seq == 0, L, 0)
    last_hi = jnp.where(t_in_seq == tiles_per_seq - 1, 2 * L, 3 * L)
    n_blk = T // L
    zero = jnp.zeros((), BF16)
    scale = A_HEAD_DIM ** -0.5
    o_cols = []
    for g in range(A_KV_HEADS):
        gs = slice(g * LANES, (g + 1) * LANES)
        kd = k_ext[:, gs]
        vd = v_ext[:, gs]
        k_lo = jnp.where(low, kd, zero)
        k_hi = jnp.where(low, zero, kd)
        o_rows = []
        for i in range(n_blk):
            qt = q_ref[i * L:(i + 1) * L, gs]
            ks = slice(i * L, (i + 3) * L)
            mask = band
            if i == 0:
                mask = mask & (kj >= first_lo)
            if i == n_blk - 1:
                mask = mask & (kj < last_hi)
            outs = []
            for sub, kk in enumerate((k_lo, k_hi)):
                sink = sink_ref[2 * g + sub:2 * g + sub + 1, 0:1]
                s = lax.dot_general(qt, kk[ks], (((1,), (1,)), ((), ())),
                                    preferred_element_type=F32) * scale
                s = jnp.where(mask, s, -jnp.inf)
                m = jnp.maximum(jnp.max(s, axis=-1, keepdims=True), sink)
                p = jnp.exp(s - m)
                p = p * (1.0 / (jnp.sum(p, axis=-1, keepdims=True) + jnp.exp(sink - m)))
                outs.append(jnp.dot(p.astype(BF16), vd[ks], preferred_element_type=F32))
            o_rows.append(jnp.where(low, outs[0], outs[1]))
        o_cols.append(jnp.concatenate(o_rows, axis=0))
    oa = jnp.concatenate(o_cols, axis=1).astype(BF16)
    y_a = jnp.dot(oa, wpa_ref[...], preferred_element_type=F32)

    gates = jnp.dot(xb, wg_ref[...], preferred_element_type=F32)
    y = _sigmoid(gates[:, :D_MODEL]) * y_m + _sigmoid(gates[:, D_MODEL:]) * y_a
    out_ref[...] = x + jnp.dot(y.astype(BF16), wo_ref[...], preferred_element_type=F32)


def _mixer_out(x2, seq_len, nw, hf, hb, mqkvo, mhw, aqkv, sink, wg, wpm, wpa, wo):
    n_tok = x2.shape[0]
    T, L = TOKEN_TILE, A_BLOCK
    tiles_per_seq = seq_len // T
    r = T // L
    n_lblk = n_tok // L
    full = lambda a: _resident(a.shape)
    tile = lambda width, j: pl.BlockSpec((T, width), lambda i: (i, j))
    prev = lambda j: pl.BlockSpec((L, A_WIDTH), lambda i: (jnp.maximum(i * r - 1, 0), j))
    nxt = lambda j: pl.BlockSpec((L, A_WIDTH), lambda i: (jnp.minimum((i + 1) * r, n_lblk - 1), j))
    return pl.pallas_call(
        functools.partial(_mixer_out_kernel, tiles_per_seq=tiles_per_seq),
        grid=(n_tok // T,),
        in_specs=[
            tile(D_MODEL, 0), full(nw),
            tile(M_WIDTH, 0), tile(M_WIDTH, 0), tile(M_WIDTH, 3), full(mhw),
            tile(A_WIDTH, 0),
            prev(1), tile(A_WIDTH, 1), nxt(1),
            prev(2), tile(A_WIDTH, 2), nxt(2),
            full(sink), full(wg), full(wpm), full(wpa), full(wo),
        ],
        out_specs=tile(D_MODEL, 0),
        out_shape=jax.ShapeDtypeStruct((n_tok, D_MODEL), F32),
        compiler_params=pltpu.CompilerParams(
            dimension_semantics=("arbitrary",), vmem_limit_bytes=VMEM_LIMIT),
        name="mixer_out",
    )(x2, nw, hf, hb, mqkvo, mhw, aqkv, aqkv, aqkv, aqkv, aqkv, aqkv, aqkv,
      sink, wg, wpm, wpa, wo)


def _conv_ffn_kernel(hp_ref, hc_ref, hn_ref, nw_ref, wua_ref, wub_ref, cw_ref, cb_ref, wd_ref,
                     nfw_ref, out_ref, xn_ref, *, tiles_per_seq):
    T, HALO, C = TOKEN_TILE, FFN_HALO, FFN_CHUNK
    t_in_seq = pl.program_id(0) % tiles_per_seq
    nw = nw_ref[...]
    x = hc_ref[...]
    keep_prev = jnp.where(t_in_seq == 0, 0.0, 1.0)
    keep_next = jnp.where(t_in_seq == tiles_per_seq - 1, 0.0, 1.0)
    xn_ref[0:HALO, :] = (_rmsnorm(hp_ref[...], nw) * keep_prev).astype(BF16)
    xn_ref[HALO:HALO + T, :] = _rmsnorm(x, nw).astype(BF16)
    xn_ref[HALO + T:, :] = (_rmsnorm(hn_ref[...], nw) * keep_next).astype(BF16)
    xe = xn_ref[...]
    rows = T + 2 * HALO

    def conv(u, cw, cb):
        up = pltpu.roll(u, 1, 0)[HALO:HALO + T]
        un = pltpu.roll(u, rows - 1, 0)[HALO:HALO + T]
        return up * cw[0:1] + u[HALO:HALO + T] * cw[1:2] + un * cw[2:3] + cb

    acc = jnp.zeros((T, D_MODEL), F32)
    for c in range(D_FF // C):
        cs = slice(c * C, (c + 1) * C)
        ua = jnp.dot(xe, wua_ref[:, cs], preferred_element_type=F32)
        ub = jnp.dot(xe, wub_ref[:, cs], preferred_element_type=F32)
        a = conv(ua, cw_ref[:, cs], cb_ref[:, cs])
        b = conv(ub, cw_ref[:, D_FF + c * C:D_FF + (c + 1) * C], cb_ref[:, D_FF + c * C:D_FF + (c + 1) * C])
        z = (a * _sigmoid(a) * b).astype(BF16)
        acc = acc + jnp.dot(z, wd_ref[cs, :], preferred_element_type=F32)
    out_ref[...] = _rmsnorm(x + acc, nfw_ref[...])


def _conv_ffn(h1, seq_len, nw, wua, wub, cw, cb, wd, nfw):
    n_tok = h1.shape[0]
    T, HALO = TOKEN_TILE, FFN_HALO
    tiles_per_seq = seq_len // T
    r = T // HALO
    n_hblk = n_tok // HALO
    full = lambda a: _resident(a.shape)
    return pl.pallas_call(
        functools.partial(_conv_ffn_kernel, tiles_per_seq=tiles_per_seq),
        grid=(n_tok // T,),
        in_specs=[
            pl.BlockSpec((HALO, D_MODEL), lambda i: (jnp.maximum(i * r - 1, 0), 0)),
            pl.BlockSpec((T, D_MODEL), lambda i: (i, 0)),
            pl.BlockSpec((HALO, D_MODEL), lambda i: (jnp.minimum((i + 1) * r, n_hblk - 1), 0)),
            full(nw), full(wua), full(wub), full(cw), full(cb), full(wd), full(nfw),
        ],
        out_specs=pl.BlockSpec((T, D_MODEL), lambda i: (i, 0)),
        out_shape=jax.ShapeDtypeStruct((n_tok, D_MODEL), F32),
        scratch_shapes=[pltpu.VMEM((T + 2 * HALO, D_MODEL), BF16)],
        compiler_params=pltpu.CompilerParams(
            dimension_semantics=("arbitrary",), vmem_limit_bytes=VMEM_LIMIT),
        name="conv_ffn",
    )(h1, h1, h1, nw, wua, wub, cw, cb, wd, nfw)


def _rope_tables(seq_len):
    half = ROT_DIM // 2
    inv = ROPE_THETA ** (-jnp.arange(half, dtype=F32) / half)
    ang = jnp.arange(seq_len, dtype=F32)[:, None] * inv[None, :]
    cos, sin = jnp.cos(ang), jnp.sin(ang)
    pad = jnp.zeros((seq_len, A_HEAD_DIM - ROT_DIM), F32)
    zero = jnp.zeros_like(sin)
    cos_h = jnp.concatenate([cos, cos, pad + 1.0], axis=1)
    sa_h = jnp.concatenate([-sin, zero, pad], axis=1)
    sb_h = jnp.concatenate([zero, sin, pad], axis=1)
    two = lambda t: jnp.concatenate([t, t], axis=1)
    return two(cos_h), two(sa_h), two(sb_h)


def _encoder(x, p):
    n_seq, seq_len, _ = x.shape
    x2 = x.reshape(n_seq * seq_len, D_MODEL)
    cos_t, sa_t, sb_t = _rope_tables(seq_len)
    mqkvo, gates, aqkv = _in_proj(x2, seq_len, p["norm1_w"], p["w_m"], p["w_gate"], p["w_a"],
                                  p["gate_bias"], cos_t, sa_t, sb_t)
    hf, hb = _mlstm(mqkvo, gates, n_seq, seq_len)
    h1 = _mixer_out(x2, seq_len, p["norm1_w"], hf, hb, mqkvo, p["mh_norm_w"], aqkv, p["sink"],
                    p["w_branch_gate"], p["w_proj_m"], p["w_proj_a"], p["w_out"])
    y = _conv_ffn(h1, seq_len, p["norm2_w"], p["w_up_a"], p["w_up_b"], p["conv_w"], p["conv_b"],
                  p["w_down"], p["norm_f_w"])
    return y.reshape(n_seq, seq_len, D_MODEL)


def _prepare_params(norm1_w, w_in, i_bias, f_bias, mh_norm_w, attn_sink, w_proj_m, w_proj_a,
                    w_out, norm2_w, w_up, conv_w, conv_b, w_down, norm_f_w):
    w_in = w_in[0]
    m_end = 4 * M_WIDTH
    g_end = m_end + 4 * M_HEADS
    a_end = g_end + A_WIDTH + 2 * A_KV_WIDTH
    w_gate = jnp.pad(w_in[:, m_end:g_end], ((0, 0), (0, LANES - 4 * M_HEADS)))
    gate_bias = jnp.pad(jnp.concatenate([i_bias[0].reshape(-1), f_bias[0].reshape(-1)]),
                        (0, LANES - 4 * M_HEADS)).reshape(1, LANES)
    return {
        "norm1_w": norm1_w[0].reshape(1, D_MODEL),
        "w_m": w_in[:, :m_end].astype(BF16),
        "w_gate": w_gate.astype(BF16),
        "gate_bias": gate_bias,
        "w_a": w_in[:, g_end:a_end].astype(BF16),
        "w_branch_gate": w_in[:, a_end:].astype(BF16),
        "mh_norm_w": mh_norm_w[0].reshape(1, M_WIDTH),
        "sink": jnp.broadcast_to(attn_sink[0][:, None], (A_HEADS, LANES)),
        "w_proj_m": w_proj_m[0].astype(BF16),
        "w_proj_a": w_proj_a[0].astype(BF16),
        "w_out": w_out[0].astype(BF16),
        "norm2_w": norm2_w[0].reshape(1, D_MODEL),
        "w_up_a": w_up[0][:, :D_FF].astype(BF16),
        "w_up_b": w_up[0][:, D_FF:].astype(BF16),
        "conv_w": conv_w[0],
        "conv_b": conv_b[0].reshape(1, 2 * D_FF),
        "w_down": w_down[0].astype(BF16),
        "norm_f_w": norm_f_w.reshape(1, D_MODEL),
    }


def kernel(x_prompt, x_sample, norm1_w, w_in, i_bias, f_bias, mh_norm_w, attn_sink, w_proj_m, w_proj_a,
           w_out, norm2_w, w_up, conv_w, conv_b, w_down, norm_f_w):
    p = _prepare_params(norm1_w, w_in, i_bias, f_bias, mh_norm_w, attn_sink, w_proj_m, w_proj_a,
                        w_out, norm2_w, w_up, conv_w, conv_b, w_down, norm_f_w)
    return (_encoder(x_prompt, p), _encoder(x_sample, p))
```

```python
import functools

import jax
import jax.numpy as jnp
from jax import lax
from jax.experimental import pallas as pl
from jax.experimental.pallas import tpu as pltpu

F32 = jnp.float32
BF16 = jnp.bfloat16

D_MODEL = 1024
M_HEADS = 4
M_HEAD_DIM = 128
M_WIDTH = M_HEADS * M_HEAD_DIM
M_CHUNK = 128
A_HEADS = 8
A_KV_HEADS = 4
A_HEAD_DIM = 64
A_WIDTH = A_HEADS * A_HEAD_DIM
A_KV_WIDTH = A_KV_HEADS * A_HEAD_DIM
WINDOW = 128
A_BLOCK = 128
ROT_DIM = A_HEAD_DIM // 4
ROPE_THETA = 500000.0
D_FF = 2816
EPS = 1e-6
LOG2E = 1.4426950408889634

LANES = 128
TOKEN_TILE = 512
FFN_HALO = 16
FFN_CHUNK = 256
VMEM_LIMIT = 56 * 1024 * 1024


def _resident(shape):
    return pl.BlockSpec(shape, lambda *_: (0,) * len(shape), pipeline_mode=pl.Buffered(1))


def _rmsnorm(x, w):
    return x * lax.rsqrt(jnp.mean(x * x, axis=-1, keepdims=True) + EPS) * w


def _sigmoid(x):
    return 1.0 / (1.0 + jnp.exp(-x))


def _in_proj_kernel(x_ref, nw_ref, wm_ref, wg_ref, wa_ref, gb_ref, cos_ref, sa_ref, sb_ref,
                    m_out, g_out, a_out):
    xb = _rmsnorm(x_ref[...], nw_ref[...]).astype(BF16)
    m_out[...] = jnp.dot(xb, wm_ref[...], preferred_element_type=F32).astype(BF16)
    g_out[...] = jnp.dot(xb, wg_ref[...], preferred_element_type=F32) + gb_ref[...]
    a = jnp.dot(xb, wa_ref[...], preferred_element_type=F32)
    cos, sa, sb = cos_ref[...], sa_ref[...], sb_ref[...]
    half = ROT_DIM // 2
    lane = lax.broadcasted_iota(jnp.int32, (1, LANES), 1)
    low = lane < A_HEAD_DIM

    def rope(t):
        return t * cos + pltpu.roll(t, LANES - half, 1) * sa + pltpu.roll(t, half, 1) * sb

    def dup_heads(t):
        sw = pltpu.roll(t, A_HEAD_DIM, 1)
        return jnp.where(low, t, sw), jnp.where(low, sw, t)

    q_scale = A_HEAD_DIM ** -0.5 * LOG2E
    for j in range(A_WIDTH // LANES):
        a_out[:, j * LANES:(j + 1) * LANES] = (rope(a[:, j * LANES:(j + 1) * LANES]) * q_scale).astype(BF16)
    k_off, v_off = A_WIDTH, A_WIDTH + A_KV_WIDTH
    for j in range(A_KV_WIDTH // LANES):
        k0, k1 = dup_heads(rope(a[:, k_off + j * LANES:k_off + (j + 1) * LANES]))
        v0, v1 = dup_heads(a[:, v_off + j * LANES:v_off + (j + 1) * LANES])
        ko = A_WIDTH + 2 * j * LANES
        vo = 2 * A_WIDTH + 2 * j * LANES
        a_out[:, ko:ko + LANES] = k0.astype(BF16)
        a_out[:, ko + LANES:ko + 2 * LANES] = k1.astype(BF16)
        a_out[:, vo:vo + LANES] = v0.astype(BF16)
        a_out[:, vo + LANES:vo + 2 * LANES] = v1.astype(BF16)


def _in_proj(x2, seq_len, nw, wm, wg, wa, gb, cos_t, sa_t, sb_t):
    n_tok = x2.shape[0]
    T = TOKEN_TILE
    tiles_per_seq = seq_len // T
    full = _resident
    rope_spec = pl.BlockSpec((T, LANES), lambda i: (i % tiles_per_seq, 0))
    return pl.pallas_call(
        _in_proj_kernel,
        grid=(n_tok // T,),
        in_specs=[
            pl.BlockSpec((T, D_MODEL), lambda i: (i, 0)),
            full((1, D_MODEL)),
            full(wm.shape), full(wg.shape), full(wa.shape), full((1, LANES)),
            rope_spec, rope_spec, rope_spec,
        ],
        out_specs=[
            pl.BlockSpec((T, 4 * M_WIDTH), lambda i: (i, 0)),
            pl.BlockSpec((T, LANES), lambda i: (i, 0)),
            pl.BlockSpec((T, 3 * A_WIDTH), lambda i: (i, 0)),
        ],
        out_shape=[
            jax.ShapeDtypeStruct((n_tok, 4 * M_WIDTH), BF16),
            jax.ShapeDtypeStruct((n_tok, LANES), F32),
            jax.ShapeDtypeStruct((n_tok, 3 * A_WIDTH), BF16),
        ],
        compiler_params=pltpu.CompilerParams(
            dimension_semantics=("arbitrary",), vmem_limit_bytes=VMEM_LIMIT),
        name="in_proj",
    )(x2, nw, wm, wg, wa, gb, cos_t, sa_t, sb_t)


def _log_sigmoid(x):
    return jnp.minimum(x, 0.0) - jnp.log(1.0 + jnp.exp(-jnp.abs(x)))


def _cumsum_lanes(x):
    lane = lax.broadcasted_iota(jnp.int32, x.shape, 1)
    sh = 1
    while sh < x.shape[1]:
        x = x + jnp.where(lane >= sh, pltpu.roll(x, sh, 1), 0.0)
        sh *= 2
    return x


def _mlstm_kernel(qf_ref, kf_ref, vf_ref, gf_ref, qb_ref, kb_ref, vb_ref, gb_ref,
                  hf_out, hb_out, cn_ref, m_ref):
    L, DH, H = M_CHUNK, M_HEAD_DIM, M_HEADS
    scale = DH ** -0.5

    @pl.when(pl.program_id(1) == 0)
    def _():
        cn_ref[...] = jnp.zeros_like(cn_ref)
        m_ref[...] = jnp.zeros_like(m_ref)

    row = lax.broadcasted_iota(jnp.int32, (L, L), 0)
    col = lax.broadcasted_iota(jnp.int32, (L, L), 1)
    lane = lax.broadcasted_iota(jnp.int32, (L, DH), 1)
    ones_col = jnp.where(lane == 0, 1.0, 0.0).astype(BF16)

    dirs = ((qf_ref, kf_ref, vf_ref, gf_ref, hf_out), (qb_ref, kb_ref, vb_ref, gb_ref, hb_out))
    for d, (q_ref, k_ref, v_ref, g_ref, h_out) in enumerate(dirs):
        gt = g_ref[...].T
        i_rows = gt[0:2 * H]
        lf_rows = _log_sigmoid(gt[2 * H:4 * H])
        cum = _cumsum_lanes(lf_rows)
        tot = cum[:, L - 1:L]
        b_rows = cum if d == 0 else tot - cum + lf_rows
        a_rows = tot - b_rows + i_rows
        m_loc_all = jnp.max(a_rows, axis=1, keepdims=True)
        packed = jnp.concatenate(
            [b_rows, i_rows, jnp.zeros((L - 4 * H, L), F32)], axis=0)
        cols = packed.T
        causal = (col <= row) if d == 0 else (col >= row)

        for h in range(H):
            r = d * H + h
            sl = slice(h * DH, (h + 1) * DH)
            q = q_ref[:, sl]
            k = k_ref[:, sl]
            v = v_ref[:, sl]
            b_row, i_row = b_rows[r:r + 1], i_rows[r:r + 1]
            b_col, i_col = cols[:, r:r + 1], cols[:, 2 * H + r:2 * H + r + 1]
            b_tot = tot[r:r + 1]
            m_loc = m_loc_all[r:r + 1]
            m_prev = m_ref[r:r + 1, :]
            cn_prev = cn_ref[r]

            inter_log = b_col + m_prev[:, 0:1]
            dmat = jnp.where(causal, b_col - b_row + i_row, -jnp.inf)
            m_t = jnp.maximum(inter_log, jnp.max(dmat, axis=1, keepdims=True))
            dw = jnp.exp(dmat - m_t)
            inter_w = jnp.exp(inter_log - m_t)
            s = lax.dot_general(q, k, (((1,), (1,)), ((), ())), preferred_element_type=F32)
            p = (s * scale * dw).astype(BF16)
            qi = (q.astype(F32) * inter_w).astype(BF16)
            v_aug = jnp.concatenate([v, ones_col], axis=1)
            lhs = jnp.concatenate([p, qi], axis=1)
            rhs = jnp.concatenate([v_aug, cn_prev.astype(BF16)], axis=0)
            out = jnp.dot(lhs, rhs, preferred_element_type=F32)
            num = out[:, :DH]
            den = out[:, DH:DH + 1]
            h_out[:, sl] = num / jnp.maximum(jnp.abs(den), jnp.exp(-m_t))

            a_col = b_tot - b_col + i_col
            wgt = jnp.exp(a_col - m_loc) * scale
            kw = (k.astype(F32) * wgt).astype(BF16)
            cn_loc = lax.dot_general(kw, v_aug, (((0,), (0,)), ((), ())),
                                     preferred_element_type=F32)
            m_new = jnp.maximum(b_tot + m_prev, m_loc)
            s_prev = jnp.exp(b_tot + m_prev - m_new)
            s_loc = jnp.exp(m_loc - m_new)
            cn_ref[r] = s_prev[:, 0:1] * cn_prev + s_loc[:, 0:1] * cn_loc
            m_ref[r:r + 1, :] = m_new


def _mlstm(mqkvo, gates, n_seq, seq_len):
    n_tok = mqkvo.shape[0]
    L = M_CHUNK
    n_chunks = seq_len // L
    fwd = lambda j: (lambda s, c: (s * n_chunks + c, j))
    bwd = lambda j: (lambda s, c: (s * n_chunks + n_chunks - 1 - c, j))
    blk = lambda im: pl.BlockSpec((L, M_WIDTH), im)
    gblk = lambda im: pl.BlockSpec((L, LANES), im)
    return pl.pallas_call(
        _mlstm_kernel,
        grid=(n_seq, n_chunks),
        in_specs=[blk(fwd(0)), blk(fwd(1)), blk(fwd(2)), gblk(fwd(0)),
                  blk(bwd(0)), blk(bwd(1)), blk(bwd(2)), gblk(bwd(0))],
        out_specs=[blk(fwd(0)), blk(bwd(0))],
        out_shape=[jax.ShapeDtypeStruct((n_tok, M_WIDTH), F32)] * 2,
        scratch_shapes=[
            pltpu.VMEM((2 * M_HEADS, M_HEAD_DIM, 2 * M_HEAD_DIM), F32),
            pltpu.VMEM((2 * M_HEADS, LANES), F32),
        ],
        compiler_params=pltpu.CompilerParams(
            dimension_semantics=("arbitrary", "arbitrary"), vmem_limit_bytes=VMEM_LIMIT),
        name="mlstm",
    )(mqkvo, mqkvo, mqkvo, gates, mqkvo, mqkvo, mqkvo, gates)


def _mixer_out_kernel(x_ref, nw_ref, hf_ref, hb_ref, mo_ref, mhw_ref, q_ref,
                      kp_ref, kc_ref, kn_ref, vp_ref, vc_ref, vn_ref, sink_ref,
                      wg_ref, wpm_ref, wpa_ref, wo_ref, out_ref, *, tiles_per_seq):
    T, L = TOKEN_TILE, A_BLOCK
    t_in_seq = pl.program_id(0) % tiles_per_seq
    x = x_ref[...]
    xb = _rmsnorm(x, nw_ref[...]).astype(BF16)

    k_ext = jnp.concatenate([kp_ref[...], kc_ref[...], kn_ref[...]], axis=0)
    v_ext = jnp.concatenate([vp_ref[...], vc_ref[...], vn_ref[...]], axis=0)
    lane = lax.broadcasted_iota(jnp.int32, (1, LANES), 1)
    low = lane < A_HEAD_DIM
    n_blk = T // L
    W3 = 3 * L
    qi = lax.broadcasted_iota(jnp.int32, (T, L), 0)
    kj = lax.broadcasted_iota(jnp.int32, (T, L), 1)
    blk = qi // L
    ql = qi - blk * L
    has_prev = t_in_seq > 0
    has_next = t_in_seq < tiles_per_seq - 1
    bias_prev = jnp.where((kj >= ql) & ((blk > 0) | has_prev), 0.0, -jnp.inf)
    bias_next = jnp.where((kj <= ql) & ((blk < n_blk - 1) | has_next), 0.0, -jnp.inf)
    zero = jnp.zeros((), BF16)
    sinks2 = sink_ref[...] * LOG2E

    def scores(g):
        gs = slice(g * LANES, (g + 1) * LANES)
        kd = k_ext[:, gs]
        k_lo, k_hi = jnp.where(low, kd, zero), jnp.where(low, zero, kd)
        rows = []
        for i in range(n_blk):
            ks = slice(i * L, (i + 3) * L)
            kbd = jnp.concatenate([k_lo[ks], k_hi[ks]], axis=0)
            rows.append(lax.dot_general(q_ref[i * L:(i + 1) * L, gs], kbd, (((1,), (1,)), ((), ())),
                                        preferred_element_type=F32))
        return jnp.concatenate(rows, axis=0)

    def softmax(g, s):
        p_parts, invs = [], []
        for sub in range(2):
            sink = sinks2[2 * g + sub:2 * g + sub + 1, 0:1]
            c0 = s[:, sub * W3:sub * W3 + L] + bias_prev
            c1 = s[:, sub * W3 + L:sub * W3 + 2 * L]
            c2 = s[:, sub * W3 + 2 * L:(sub + 1) * W3] + bias_next
            m = jnp.maximum(jnp.max(jnp.maximum(jnp.maximum(c0, c1), c2), axis=-1, keepdims=True), sink)
            p0, p1, p2 = jnp.exp2(c0 - m), jnp.exp2(c1 - m), jnp.exp2(c2 - m)
            denom = jnp.sum(p0 + p1 + p2, axis=-1, keepdims=True) + jnp.exp2(sink - m)
            invs.append(1.0 / denom)
            p_parts += [p0.astype(BF16), p1.astype(BF16), p2.astype(BF16)]
        return jnp.concatenate(p_parts, axis=1), jnp.where(low, invs[0], invs[1])

    def weighted_values(g, p, inv):
        gs = slice(g * LANES, (g + 1) * LANES)
        vd = v_ext[:, gs]
        v_lo, v_hi = jnp.where(low, vd, zero), jnp.where(low, zero, vd)
        rows = []
        for i in range(n_blk):
            ks = slice(i * L, (i + 3) * L)
            vbd = jnp.concatenate([v_lo[ks], v_hi[ks]], axis=0)
            rows.append(jnp.dot(p[i * L:(i + 1) * L], vbd, preferred_element_type=F32))
        return (jnp.concatenate(rows, axis=0) * inv).astype(BF16)

    s = scores(0)

    hsum = hf_ref[...] + hb_ref[...]
    parts = []
    for h in range(M_HEADS):
        hh = hsum[:, h * M_HEAD_DIM:(h + 1) * M_HEAD_DIM]
        hh = hh - jnp.mean(hh, axis=-1, keepdims=True)
        parts.append(hh * lax.rsqrt(jnp.mean(hh * hh, axis=-1, keepdims=True) + EPS))
    hn = jnp.concatenate(parts, axis=1) * mhw_ref[...]
    hg = (_sigmoid(mo_ref[...].astype(F32)) * hn).astype(BF16)
    y_m = jnp.dot(hg, wpm_ref[...], preferred_element_type=F32)

    o_cols = []
    p, inv = softmax(0, s)
    s = scores(1)
    o_cols.append(weighted_values(0, p, inv))
    gate_m = jnp.dot(xb, wg_ref[:, :D_MODEL], preferred_element_type=F32)
    p, inv = softmax(1, s)
    s = scores(2)
    o_cols.append(weighted_values(1, p, inv))
    gate_a = jnp.dot(xb, wg_ref[:, D_MODEL:], preferred_element_type=F32)
    p, inv = softmax(2, s)
    s = scores(3)
    o_cols.append(weighted_values(2, p, inv))
    y_m = _sigmoid(gate_m) * y_m
    p, inv = softmax(3, s)
    o_cols.append(weighted_values(3, p, inv))
    oa = jnp.concatenate(o_cols, axis=1)
    y_a = jnp.dot(oa, wpa_ref[...], preferred_element_type=F32)

    y = y_m + _sigmoid(gate_a) * y_a
    out_ref[...] = x + jnp.dot(y.astype(BF16), wo_ref[...], preferred_element_type=F32)


def _mixer_out(x2, seq_len, nw, hf, hb, mqkvo, mhw, aqkv, sink, wg, wpm, wpa, wo):
    n_tok = x2.shape[0]
    T, L = TOKEN_TILE, A_BLOCK
    tiles_per_seq = seq_len // T
    r = T // L
    n_lblk = n_tok // L
    full = lambda a: _resident(a.shape)
    tile = lambda width, j: pl.BlockSpec((T, width), lambda i: (i, j))
    prev = lambda j: pl.BlockSpec((L, A_WIDTH), lambda i: (jnp.maximum(i * r - 1, 0), j))
    nxt = lambda j: pl.BlockSpec((L, A_WIDTH), lambda i: (jnp.minimum((i + 1) * r, n_lblk - 1), j))
    return pl.pallas_call(
        functools.partial(_mixer_out_kernel, tiles_per_seq=tiles_per_seq),
        grid=(n_tok // T,),
        in_specs=[
            tile(D_MODEL, 0), full(nw),
            tile(M_WIDTH, 0), tile(M_WIDTH, 0), tile(M_WIDTH, 3), full(mhw),
            tile(A_WIDTH, 0),
            prev(1), tile(A_WIDTH, 1), nxt(1),
            prev(2), tile(A_WIDTH, 2), nxt(2),
            full(sink), full(wg), full(wpm), full(wpa), full(wo),
        ],
        out_specs=tile(D_MODEL, 0),
        out_shape=jax.ShapeDtypeStruct((n_tok, D_MODEL), F32),
        compiler_params=pltpu.CompilerParams(
            dimension_semantics=("arbitrary",), vmem_limit_bytes=VMEM_LIMIT),
        name="mixer_out",
    )(x2, nw, hf, hb, mqkvo, mhw, aqkv, aqkv, aqkv, aqkv, aqkv, aqkv, aqkv,
      sink, wg, wpm, wpa, wo)


def _conv_ffn_kernel(hp_ref, hc_ref, hn_ref, nw_ref, wua_ref, wub_ref, cw_ref, cb_ref, wd_ref,
                     nfw_ref, out_ref, xn_ref, z_ref, *, tiles_per_seq):
    T, HALO, C = TOKEN_TILE, FFN_HALO, FFN_CHUNK
    t_in_seq = pl.program_id(0) % tiles_per_seq
    nw = nw_ref[...]
    x = hc_ref[...]
    keep_prev = jnp.where(t_in_seq == 0, 0.0, 1.0)
    keep_next = jnp.where(t_in_seq == tiles_per_seq - 1, 0.0, 1.0)
    xn_ref[0:HALO, :] = (_rmsnorm(hp_ref[...], nw) * keep_prev).astype(BF16)
    xn_ref[HALO:HALO + T, :] = _rmsnorm(x, nw).astype(BF16)
    xn_ref[HALO + T:, :] = (_rmsnorm(hn_ref[...], nw) * keep_next).astype(BF16)
    xe = xn_ref[...]
    rows = T + 2 * HALO

    def conv(u, cw, cb):
        up = pltpu.roll(u, 1, 0)[HALO:HALO + T]
        un = pltpu.roll(u, rows - 1, 0)[HALO:HALO + T]
        return up * cw[0:1] + u[HALO:HALO + T] * cw[1:2] + un * cw[2:3] + cb

    def up(c):
        cs = slice(c * C, (c + 1) * C)
        return (jnp.dot(xe, wua_ref[:, cs], preferred_element_type=F32),
                jnp.dot(xe, wub_ref[:, cs], preferred_element_type=F32))

    def act(c, ua, ub):
        cs = slice(c * C, (c + 1) * C)
        gs = slice(D_FF + c * C, D_FF + (c + 1) * C)
        a = conv(ua, cw_ref[:, cs], cb_ref[:, cs])
        b = conv(ub, cw_ref[:, gs], cb_ref[:, gs])
        return (a * _sigmoid(a) * b).astype(BF16)

    n_chunks = D_FF // C
    u = up(0)
    for c in range(n_chunks):
        u_next = up(c + 1) if c + 1 < n_chunks else None
        z_ref[:, c * C:(c + 1) * C] = act(c, *u)
        u = u_next
    ffn = jnp.dot(z_ref[...], wd_ref[...], preferred_element_type=F32)
    out_ref[...] = _rmsnorm(x + ffn, nfw_ref[...])


def _conv_ffn(h1, seq_len, nw, wua, wub, cw, cb, wd, nfw):
    n_tok = h1.shape[0]
    T, HALO = TOKEN_TILE, FFN_HALO
    tiles_per_seq = seq_len // T
    r = T // HALO
    n_hblk = n_tok // HALO
    full = lambda a: _resident(a.shape)
    return pl.pallas_call(
        functools.partial(_conv_ffn_kernel, tiles_per_seq=tiles_per_seq),
        grid=(n_tok // T,),
        in_specs=[
            pl.BlockSpec((HALO, D_MODEL), lambda i: (jnp.maximum(i * r - 1, 0), 0)),
            pl.BlockSpec((T, D_MODEL), lambda i: (i, 0)),
            pl.BlockSpec((HALO, D_MODEL), lambda i: (jnp.minimum((i + 1) * r, n_hblk - 1), 0)),
            full(nw), full(wua), full(wub), full(cw), full(cb), full(wd), full(nfw),
        ],
        out_specs=pl.BlockSpec((T, D_MODEL), lambda i: (i, 0)),
        out_shape=jax.ShapeDtypeStruct((n_tok, D_MODEL), F32),
        scratch_shapes=[pltpu.VMEM((T + 2 * HALO, D_MODEL), BF16), pltpu.VMEM((T, D_FF), BF16)],
        compiler_params=pltpu.CompilerParams(
            dimension_semantics=("arbitrary",), vmem_limit_bytes=VMEM_LIMIT),
        name="conv_ffn",
    )(h1, h1, h1, nw, wua, wub, cw, cb, wd, nfw)


def _rope_tables(seq_len):
    half = ROT_DIM // 2
    inv = ROPE_THETA ** (-jnp.arange(half, dtype=F32) / half)
    ang = jnp.arange(seq_len, dtype=F32)[:, None] * inv[None, :]
    cos, sin = jnp.cos(ang), jnp.sin(ang)
    pad = jnp.zeros((seq_len, A_HEAD_DIM - ROT_DIM), F32)
    zero = jnp.zeros_like(sin)
    cos_h = jnp.concatenate([cos, cos, pad + 1.0], axis=1)
    sa_h = jnp.concatenate([-sin, zero, pad], axis=1)
    sb_h = jnp.concatenate([zero, sin, pad], axis=1)
    two = lambda t: jnp.concatenate([t, t], axis=1)
    return two(cos_h), two(sa_h), two(sb_h)


def _encoder(x, p):
    n_seq, seq_len, _ = x.shape
    x2 = x.reshape(n_seq * seq_len, D_MODEL)
    cos_t, sa_t, sb_t = _rope_tables(seq_len)
    mqkvo, gates, aqkv = _in_proj(x2, seq_len, p["norm1_w"], p["w_m"], p["w_gate"], p["w_a"],
                                  p["gate_bias"], cos_t, sa_t, sb_t)
    hf, hb = _mlstm(mqkvo, gates, n_seq, seq_len)
    h1 = _mixer_out(x2, seq_len, p["norm1_w"], hf, hb, mqkvo, p["mh_norm_w"], aqkv, p["sink"],
                    p["w_branch_gate"], p["w_proj_m"], p["w_proj_a"], p["w_out"])
    y = _conv_ffn(h1, seq_len, p["norm2_w"], p["w_up_a"], p["w_up_b"], p["conv_w"], p["conv_b"],
                  p["w_down"], p["norm_f_w"])
    return y.reshape(n_seq, seq_len, D_MODEL)


def _prepare_params(norm1_w, w_in, i_bias, f_bias, mh_norm_w, attn_sink, w_proj_m, w_proj_a,
                    w_out, norm2_w, w_up, conv_w, conv_b, w_down, norm_f_w):
    w_in = w_in[0]
    m_end = 4 * M_WIDTH
    g_end = m_end + 4 * M_HEADS
    a_end = g_end + A_WIDTH + 2 * A_KV_WIDTH
    w_gate = jnp.pad(w_in[:, m_end:g_end], ((0, 0), (0, LANES - 4 * M_HEADS)))
    gate_bias = jnp.pad(jnp.concatenate([i_bias[0].reshape(-1), f_bias[0].reshape(-1)]),
                        (0, LANES - 4 * M_HEADS)).reshape(1, LANES)
    return {
        "norm1_w": norm1_w[0].reshape(1, D_MODEL),
        "w_m": w_in[:, :m_end].astype(BF16),
        "w_gate": w_gate.astype(BF16),
        "gate_bias": gate_bias,
        "w_a": w_in[:, g_end:a_end].astype(BF16),
        "w_branch_gate": w_in[:, a_end:].astype(BF16),
        "mh_norm_w": mh_norm_w[0].reshape(1, M_WIDTH),
        "sink": jnp.broadcast_to(attn_sink[0][:, None], (A_HEADS, LANES)),
        "w_proj_m": w_proj_m[0].astype(BF16),
        "w_proj_a": w_proj_a[0].astype(BF16),
        "w_out": w_out[0].astype(BF16),
        "norm2_w": norm2_w[0].reshape(1, D_MODEL),
        "w_up_a": w_up[0][:, :D_FF].astype(BF16),
        "w_up_b": w_up[0][:, D_FF:].astype(BF16),
        "conv_w": conv_w[0],
        "conv_b": conv_b[0].reshape(1, 2 * D_FF),
        "w_down": w_down[0].astype(BF16),
        "norm_f_w": norm_f_w.reshape(1, D_MODEL),
    }


def kernel(x_prompt, x_sample, norm1_w, w_in, i_bias, f_bias, mh_norm_w, attn_sink, w_proj_m, w_proj_a,
           w_out, norm2_w, w_up, conv_w, conv_b, w_down, norm_f_w):
    p = _prepare_params(norm1_w, w_in, i_bias, f_bias, mh_norm_w, attn_sink, w_proj_m, w_proj_a,
                        w_out, norm2_w, w_up, conv_w, conv_b, w_down, norm_f_w)
    return (_encoder(x_prompt, p), _encoder(x_sample, p))
```

```python
import functools

import jax
import jax.numpy as jnp
from jax import lax
from jax.experimental import pallas as pl
from jax.experimental.pallas import tpu as pltpu

F32 = jnp.float32
BF16 = jnp.bfloat16

D_MODEL = 1024
M_HEADS = 4
M_HEAD_DIM = 128
M_WIDTH = M_HEADS * M_HEAD_DIM
M_CHUNK = 128
A_HEADS = 8
A_KV_HEADS = 4
A_HEAD_DIM = 64
A_WIDTH = A_HEADS * A_HEAD_DIM
A_KV_WIDTH = A_KV_HEADS * A_HEAD_DIM
WINDOW = 128
A_BLOCK = 128
ROT_DIM = A_HEAD_DIM // 4
ROPE_THETA = 500000.0
D_FF = 2816
EPS = 1e-6
LOG2E = 1.4426950408889634

LANES = 128
TOKEN_TILE = 512
FFN_HALO = 16
FFN_CHUNK = 256
M_AUG = 16
GS_ROWS = 5 * 2 * M_HEADS
SCAN_CHUNKS = 8
MLSTM_GROUP = 4
VMEM_LIMIT = 56 * 1024 * 1024


def _resident(shape):
    return pl.BlockSpec(shape, lambda *_: (0,) * len(shape), pipeline_mode=pl.Buffered(1))


def _rmsnorm(x, w):
    return x * lax.rsqrt(jnp.mean(x * x, axis=-1, keepdims=True) + EPS) * w


def _sigmoid(x):
    return 1.0 / (1.0 + jnp.exp(-x))


def _in_proj_kernel(x_ref, nw_ref, wk_ref, wt_ref, wgt_ref, wa_ref, gb_ref, cos_ref, sa_ref, sb_ref,
                    k_out, t_out, g_out, a_out):
    T = x_ref.shape[0]
    xb = _rmsnorm(x_ref[...], nw_ref[...]).astype(BF16)
    nt = (((1,), (1,)), ((), ()))
    k_out[...] = jnp.dot(xb, wk_ref[...], preferred_element_type=F32).astype(BF16)
    tt = lax.dot_general(wt_ref[...], xb, nt, preferred_element_type=F32)
    t_out[0:M_WIDTH, :] = (tt[0:M_WIDTH] * M_HEAD_DIM ** -0.5).astype(BF16)
    t_out[M_WIDTH:, :] = tt[M_WIDTH:].astype(BF16)
    gt = lax.dot_general(wgt_ref[...], xb, nt, preferred_element_type=F32)
    g_out[...] = gt + jnp.concatenate([gb_ref[...]] * (T // LANES), axis=1)
    a = jnp.dot(xb, wa_ref[...], preferred_element_type=F32)
    cos, sa, sb = cos_ref[...], sa_ref[...], sb_ref[...]
    half = ROT_DIM // 2
    lane = lax.broadcasted_iota(jnp.int32, (1, LANES), 1)
    low = lane < A_HEAD_DIM

    def rope(t):
        return t * cos + pltpu.roll(t, LANES - half, 1) * sa + pltpu.roll(t, half, 1) * sb

    def dup_heads(t):
        sw = pltpu.roll(t, A_HEAD_DIM, 1)
        return jnp.where(low, t, sw), jnp.where(low, sw, t)

    q_scale = A_HEAD_DIM ** -0.5 * LOG2E
    for j in range(A_WIDTH // LANES):
        a_out[:, j * LANES:(j + 1) * LANES] = (rope(a[:, j * LANES:(j + 1) * LANES]) * q_scale).astype(BF16)
    k_off, v_off = A_WIDTH, A_WIDTH + A_KV_WIDTH
    for j in range(A_KV_WIDTH // LANES):
        k0, k1 = dup_heads(rope(a[:, k_off + j * LANES:k_off + (j + 1) * LANES]))
        v0, v1 = dup_heads(a[:, v_off + j * LANES:v_off + (j + 1) * LANES])
        ko = A_WIDTH + 2 * j * LANES
        vo = 2 * A_WIDTH + 2 * j * LANES
        a_out[:, ko:ko + LANES] = k0.astype(BF16)
        a_out[:, ko + LANES:ko + 2 * LANES] = k1.astype(BF16)
        a_out[:, vo:vo + LANES] = v0.astype(BF16)
        a_out[:, vo + LANES:vo + 2 * LANES] = v1.astype(BF16)


def _in_proj(x2, seq_len, nw, wk, wt, wgt, wa, gb, cos_t, sa_t, sb_t):
    n_tok = x2.shape[0]
    T = TOKEN_TILE
    tiles_per_seq = seq_len // T
    full = lambda a: _resident(a.shape)
    rope_spec = pl.BlockSpec((T, LANES), lambda i: (i % tiles_per_seq, 0))
    n_gate = 4 * M_HEADS
    return pl.pallas_call(
        _in_proj_kernel,
        grid=(n_tok // T,),
        in_specs=[
            pl.BlockSpec((T, D_MODEL), lambda i: (i, 0)),
            full(nw), full(wk), full(wt), full(wgt), full(wa), full(gb),
            rope_spec, rope_spec, rope_spec,
        ],
        out_specs=[
            pl.BlockSpec((T, M_WIDTH), lambda i: (i, 0)),
            pl.BlockSpec((3 * M_WIDTH, T), lambda i: (0, i)),
            pl.BlockSpec((n_gate, T), lambda i: (0, i)),
            pl.BlockSpec((T, 3 * A_WIDTH), lambda i: (i, 0)),
        ],
        out_shape=[
            jax.ShapeDtypeStruct((n_tok, M_WIDTH), BF16),
            jax.ShapeDtypeStruct((3 * M_WIDTH, n_tok), BF16),
            jax.ShapeDtypeStruct((n_gate, n_tok), F32),
            jax.ShapeDtypeStruct((n_tok, 3 * A_WIDTH), BF16),
        ],
        compiler_params=pltpu.CompilerParams(
            dimension_semantics=("arbitrary",), vmem_limit_bytes=VMEM_LIMIT),
        name="in_proj",
    )(x2, nw, wk, wt, wgt, wa, gb, cos_t, sa_t, sb_t)


def _log_sigmoid(x):
    return jnp.minimum(x, 0.0) - jnp.log(1.0 + jnp.exp(-jnp.abs(x)))


def _scan_lanes(x, op, fill, reverse):
    n = x.shape[1]
    lane = lax.broadcasted_iota(jnp.int32, x.shape, 1)
    sh = 1
    while sh < n:
        if reverse:
            x = op(x, jnp.where(lane < n - sh, pltpu.roll(x, n - sh, 1), fill))
        else:
            x = op(x, jnp.where(lane >= sh, pltpu.roll(x, sh, 1), fill))
        sh *= 2
    return x


def _gate_scan_kernel(g_ref, gs_out, uc_out):
    L, R = M_CHUNK, 2 * M_HEADS
    n_chunk = g_ref.shape[1] // L
    ig = jnp.concatenate([g_ref[0:R, j * L:(j + 1) * L] for j in range(n_chunk)], axis=0)
    lf = _log_sigmoid(jnp.concatenate([g_ref[R:2 * R, j * L:(j + 1) * L] for j in range(n_chunk)], axis=0))
    rows = lax.broadcasted_iota(jnp.int32, ig.shape, 0)
    is_fwd = (rows & (R - 1)) < M_HEADS
    cum = _scan_lanes(lf, jnp.add, 0.0, False)
    tot = jnp.broadcast_to(cum[:, L - 1:L], cum.shape)
    b = jnp.where(is_fwd, cum, tot - cum + lf)
    u = ig - b
    pmax = _scan_lanes(u, jnp.maximum, -jnp.inf, False)
    cmax = jnp.where(is_fwd, pmax, _scan_lanes(u, jnp.maximum, -jnp.inf, True))
    umax = jnp.broadcast_to(pmax[:, L - 1:L], pmax.shape)
    wgt = jnp.exp(u - umax)
    mloc = tot + umax
    pad = jnp.zeros((L - R, L), F32)
    for j in range(n_chunk):
        rs = slice(j * R, (j + 1) * R)
        gs_out[j * GS_ROWS:(j + 1) * GS_ROWS, :] = jnp.concatenate(
            [b[rs], cmax[rs], wgt[rs], tot[rs], mloc[rs]], axis=0)
        uc_out[j * L:(j + 1) * L, :] = jnp.concatenate([u[rs], pad], axis=0).T


def _gate_scan(gt):
    n_tok = gt.shape[1]
    L = M_CHUNK
    blk = SCAN_CHUNKS * L
    return pl.pallas_call(
        _gate_scan_kernel,
        grid=(n_tok // blk,),
        in_specs=[pl.BlockSpec((gt.shape[0], blk), lambda i: (0, i))],
        out_specs=[pl.BlockSpec((SCAN_CHUNKS * GS_ROWS, LANES), lambda i: (i, 0)),
                   pl.BlockSpec((blk, LANES), lambda i: (i, 0))],
        out_shape=[jax.ShapeDtypeStruct((n_tok // L * GS_ROWS, LANES), F32),
                   jax.ShapeDtypeStruct((n_tok, LANES), F32)],
        compiler_params=pltpu.CompilerParams(
            dimension_semantics=("arbitrary",), vmem_limit_bytes=VMEM_LIMIT),
        name="gate_scan",
    )(gt)


def _mlstm_kernel(kf_ref, qf_ref, vf_ref, gsf_ref, ucf_ref, kb_ref, qb_ref, vb_ref, gsb_ref, ucb_ref,
                  hf_out, hb_out, cn_ref, m_ref):
    L, DH, H, G = M_CHUNK, M_HEAD_DIM, M_HEADS, MLSTM_GROUP
    R = 2 * H

    @pl.when(pl.program_id(1) == 0)
    def _():
        cn_ref[...] = jnp.zeros_like(cn_ref)
        m_ref[...] = jnp.zeros_like(m_ref)

    is_fwd = lax.broadcasted_iota(jnp.int32, (R, L), 0) < H
    si = lax.broadcasted_iota(jnp.int32, (L, L), 0)
    ti = lax.broadcasted_iota(jnp.int32, (L, L), 1)
    tri = (jnp.where(si <= ti, 0.0, -jnp.inf), jnp.where(si >= ti, 0.0, -jnp.inf))
    ones_rows = jnp.where(lax.broadcasted_iota(jnp.int32, (M_AUG, L), 0) == 0, 1.0, 0.0).astype(BF16)
    row = lambda a, r: a[r:r + 1]

    for j in range(G):
        jf, jb = j, G - 1 - j
        gsf = gsf_ref[jf * GS_ROWS:(jf + 1) * GS_ROWS, :]
        gsb = gsb_ref[jb * GS_ROWS:(jb + 1) * GS_ROWS, :]
        b, cmax, wgt, tot, mloc = (
            jnp.where(is_fwd, gsf[n * R:(n + 1) * R], gsb[n * R:(n + 1) * R]) for n in range(5))
        m_prev = m_ref[...]
        g = jnp.maximum(m_prev, cmax)
        rho = jnp.exp(cmax - g)
        omega = jnp.exp(m_prev - g)
        thr = jnp.exp(-(b + g))
        m_new = jnp.maximum(tot + m_prev, mloc)
        s_prev = jnp.exp(tot + m_prev - m_new)
        s_loc = jnp.exp(mloc - m_new)
        m_ref[...] = m_new

        def operands(r):
            d, h = divmod(r, H)
            k_ref, q_ref, v_ref, uc_ref, jj = ((kf_ref, qf_ref, vf_ref, ucf_ref, jf) if d == 0
                                               else (kb_ref, qb_ref, vb_ref, ucb_ref, jb))
            hs, ts = slice(h * DH, (h + 1) * DH), slice(jj * L, (jj + 1) * L)
            v_aug = jnp.concatenate([v_ref[hs, ts], ones_rows], axis=0)
            return k_ref[ts, hs], q_ref[hs, ts], v_aug, uc_ref[ts, r:r + 1]

        ops = [operands(r) for r in range(R)]
        s_t = [jnp.dot(k, q_t, preferred_element_type=F32) for k, q_t, _, _ in ops]
        vw = [(v_aug.astype(F32) * row(wgt, r)).astype(BF16) for r, (_, _, v_aug, _) in enumerate(ops)]
        cn_loc = [jnp.dot(vw[r], ops[r][0], preferred_element_type=F32) for r in range(R)]
        rhs = []
        for r, (k, q_t, v_aug, u_col) in enumerate(ops):
            decay = jnp.exp(u_col + (tri[r // H] - row(cmax, r)))
            p_t = (s_t[r] * decay * row(rho, r)).astype(BF16)
            q_w = (q_t.astype(F32) * row(omega, r)).astype(BF16)
            rhs.append(jnp.concatenate([p_t, q_w], axis=0))
        for r, (k, q_t, v_aug, u_col) in enumerate(ops):
            d, h = divmod(r, H)
            cn = cn_ref[r]
            lhs = jnp.concatenate([v_aug, cn.astype(BF16)], axis=1)
            out = jnp.dot(lhs, rhs[r], preferred_element_type=F32)
            rec = 1.0 / jnp.maximum(jnp.abs(out[DH:DH + 1]), row(thr, r))
            h_out, jj = (hf_out, jf) if d == 0 else (hb_out, jb)
            h_out[h * DH:(h + 1) * DH, jj * L:(jj + 1) * L] = out[:DH] * rec
            cn_ref[r] = row(s_prev, r) * cn + row(s_loc, r) * cn_loc[r]


def _mlstm(k_tok, t_feat, gs, uc, n_seq, seq_len):
    n_tok = k_tok.shape[0]
    L, G = M_CHUNK, MLSTM_GROUP
    nb = seq_len // (L * G)
    fwd = lambda s, c: s * nb + c
    bwd = lambda s, c: s * nb + nb - 1 - c
    specs = lambda at: [
        pl.BlockSpec((G * L, M_WIDTH), lambda s, c: (at(s, c), 0)),
        pl.BlockSpec((M_WIDTH, G * L), lambda s, c: (0, at(s, c))),
        pl.BlockSpec((M_WIDTH, G * L), lambda s, c: (1, at(s, c))),
        pl.BlockSpec((G * GS_ROWS, LANES), lambda s, c: (at(s, c), 0)),
        pl.BlockSpec((G * L, LANES), lambda s, c: (at(s, c), 0)),
    ]
    out_spec = lambda at: pl.BlockSpec((M_WIDTH, G * L), lambda s, c: (0, at(s, c)))
    return pl.pallas_call(
        _mlstm_kernel,
        grid=(n_seq, nb),
        in_specs=specs(fwd) + specs(bwd),
        out_specs=[out_spec(fwd), out_spec(bwd)],
        out_shape=[jax.ShapeDtypeStruct((M_WIDTH, n_tok), F32)] * 2,
        scratch_shapes=[
            pltpu.VMEM((2 * M_HEADS, M_HEAD_DIM + M_AUG, M_HEAD_DIM), F32),
            pltpu.VMEM((2 * M_HEADS, LANES), F32),
        ],
        compiler_params=pltpu.CompilerParams(
            dimension_semantics=("arbitrary", "arbitrary"), vmem_limit_bytes=VMEM_LIMIT),
        name="mlstm",
    )(k_tok, t_feat, t_feat, gs, uc, k_tok, t_feat, t_feat, gs, uc)


def _mixer_out_kernel(x_ref, nw_ref, hf_ref, hb_ref, mo_ref, mhw_ref, q_ref,
                      kp_ref, kc_ref, kn_ref, vp_ref, vc_ref, vn_ref, sink_ref,
                      wg_ref, wpm_ref, wpa_ref, wo_ref, out_ref, *, tiles_per_seq):
    T, L = TOKEN_TILE, A_BLOCK
    t_in_seq = pl.program_id(0) % tiles_per_seq
    x = x_ref[...]
    xb = _rmsnorm(x, nw_ref[...]).astype(BF16)

    k_ext = jnp.concatenate([kp_ref[...], kc_ref[...], kn_ref[...]], axis=0)
    v_ext = jnp.concatenate([vp_ref[...], vc_ref[...], vn_ref[...]], axis=0)
    lane = lax.broadcasted_iota(jnp.int32, (1, LANES), 1)
    low = lane < A_HEAD_DIM
    n_blk = T // L
    W3 = 3 * L
    qi = lax.broadcasted_iota(jnp.int32, (T, L), 0)
    kj = lax.broadcasted_iota(jnp.int32, (T, L), 1)
    blk = qi // L
    ql = qi - blk * L
    has_prev = t_in_seq > 0
    has_next = t_in_seq < tiles_per_seq - 1
    bias_prev = jnp.where((kj >= ql) & ((blk > 0) | has_prev), 0.0, -jnp.inf)
    bias_next = jnp.where((kj <= ql) & ((blk < n_blk - 1) | has_next), 0.0, -jnp.inf)
    zero = jnp.zeros((), BF16)
    sinks2 = sink_ref[...] * LOG2E

    def scores(g):
        gs = slice(g * LANES, (g + 1) * LANES)
        kd = k_ext[:, gs]
        k_lo, k_hi = jnp.where(low, kd, zero), jnp.where(low, zero, kd)
        rows = []
        for i in range(n_blk):
            ks = slice(i * L, (i + 3) * L)
            kbd = jnp.concatenate([k_lo[ks], k_hi[ks]], axis=0)
            rows.append(lax.dot_general(q_ref[i * L:(i + 1) * L, gs], kbd, (((1,), (1,)), ((), ())),
                                        preferred_element_type=F32))
        return jnp.concatenate(rows, axis=0)

    def softmax(g, s):
        p_parts, invs = [], []
        for sub in range(2):
            sink = sinks2[2 * g + sub:2 * g + sub + 1, 0:1]
            c0 = s[:, sub * W3:sub * W3 + L] + bias_prev
            c1 = s[:, sub * W3 + L:sub * W3 + 2 * L]
            c2 = s[:, sub * W3 + 2 * L:(sub + 1) * W3] + bias_next
            m = jnp.maximum(jnp.max(jnp.maximum(jnp.maximum(c0, c1), c2), axis=-1, keepdims=True), sink)
            p0, p1, p2 = jnp.exp2(c0 - m), jnp.exp2(c1 - m), jnp.exp2(c2 - m)
            denom = jnp.sum(p0 + p1 + p2, axis=-1, keepdims=True) + jnp.exp2(sink - m)
            invs.append(1.0 / denom)
            p_parts += [p0.astype(BF16), p1.astype(BF16), p2.astype(BF16)]
        return jnp.concatenate(p_parts, axis=1), jnp.where(low, invs[0], invs[1])

    def weighted_values(g, p, inv):
        gs = slice(g * LANES, (g + 1) * LANES)
        vd = v_ext[:, gs]
        v_lo, v_hi = jnp.where(low, vd, zero), jnp.where(low, zero, vd)
        rows = []
        for i in range(n_blk):
            ks = slice(i * L, (i + 3) * L)
            vbd = jnp.concatenate([v_lo[ks], v_hi[ks]], axis=0)
            rows.append(jnp.dot(p[i * L:(i + 1) * L], vbd, preferred_element_type=F32))
        return (jnp.concatenate(rows, axis=0) * inv).astype(BF16)

    s = scores(0)

    hsum = hf_ref[...] + hb_ref[...]
    parts = []
    for h in range(M_HEADS):
        hs = slice(h * M_HEAD_DIM, (h + 1) * M_HEAD_DIM)
        hh = hsum[hs]
        hh = hh - jnp.mean(hh, axis=0, keepdims=True)
        hh = hh * lax.rsqrt(jnp.mean(hh * hh, axis=0, keepdims=True) + EPS)
        w = jnp.concatenate([mhw_ref[hs, :]] * (T // LANES), axis=1)
        parts.append((_sigmoid(mo_ref[hs, :].astype(F32)) * (hh * w)).astype(BF16))
    hg_t = jnp.concatenate(parts, axis=0)
    y_m = lax.dot_general(hg_t, wpm_ref[...], (((0,), (0,)), ((), ())), preferred_element_type=F32)

    o_cols = []
    p, inv = softmax(0, s)
    s = scores(1)
    o_cols.append(weighted_values(0, p, inv))
    gate_m = jnp.dot(xb, wg_ref[:, :D_MODEL], preferred_element_type=F32)
    p, inv = softmax(1, s)
    s = scores(2)
    o_cols.append(weighted_values(1, p, inv))
    gate_a = jnp.dot(xb, wg_ref[:, D_MODEL:], preferred_element_type=F32)
    p, inv = softmax(2, s)
    s = scores(3)
    o_cols.append(weighted_values(2, p, inv))
    y_m = _sigmoid(gate_m) * y_m
    p, inv = softmax(3, s)
    o_cols.append(weighted_values(3, p, inv))
    oa = jnp.concatenate(o_cols, axis=1)
    y_a = jnp.dot(oa, wpa_ref[...], preferred_element_type=F32)

    y = y_m + _sigmoid(gate_a) * y_a
    out_ref[...] = x + jnp.dot(y.astype(BF16), wo_ref[...], preferred_element_type=F32)


def _mixer_out(x2, seq_len, nw, hf, hb, t_feat, mhw, aqkv, sink, wg, wpm, wpa, wo):
    n_tok = x2.shape[0]
    T, L = TOKEN_TILE, A_BLOCK
    tiles_per_seq = seq_len // T
    r = T // L
    n_lblk = n_tok // L
    full = lambda a: _resident(a.shape)
    tile = lambda width, j: pl.BlockSpec((T, width), lambda i: (i, j))
    feat = lambda j: pl.BlockSpec((M_WIDTH, T), lambda i: (j, i))
    prev = lambda j: pl.BlockSpec((L, A_WIDTH), lambda i: (jnp.maximum(i * r - 1, 0), j))
    nxt = lambda j: pl.BlockSpec((L, A_WIDTH), lambda i: (jnp.minimum((i + 1) * r, n_lblk - 1), j))
    return pl.pallas_call(
        functools.partial(_mixer_out_kernel, tiles_per_seq=tiles_per_seq),
        grid=(n_tok // T,),
        in_specs=[
            tile(D_MODEL, 0), full(nw),
            feat(0), feat(0), feat(2), full(mhw),
            tile(A_WIDTH, 0),
            prev(1), tile(A_WIDTH, 1), nxt(1),
            prev(2), tile(A_WIDTH, 2), nxt(2),
            full(sink), full(wg), full(wpm), full(wpa), full(wo),
        ],
        out_specs=tile(D_MODEL, 0),
        out_shape=jax.ShapeDtypeStruct((n_tok, D_MODEL), F32),
        compiler_params=pltpu.CompilerParams(
            dimension_semantics=("arbitrary",), vmem_limit_bytes=VMEM_LIMIT),
        name="mixer_out",
    )(x2, nw, hf, hb, t_feat, mhw, aqkv, aqkv, aqkv, aqkv, aqkv, aqkv, aqkv,
      sink, wg, wpm, wpa, wo)


def _conv_ffn_kernel(hp_ref, hc_ref, hn_ref, nw_ref, wua_ref, wub_ref, cw_ref, cb_ref, wd_ref,
                     nfw_ref, out_ref, xn_ref, z_ref, *, tiles_per_seq):
    T, HALO, C = TOKEN_TILE, FFN_HALO, FFN_CHUNK
    t_in_seq = pl.program_id(0) % tiles_per_seq
    nw = nw_ref[...]
    x = hc_ref[...]
    keep_prev = jnp.where(t_in_seq == 0, 0.0, 1.0)
    keep_next = jnp.where(t_in_seq == tiles_per_seq - 1, 0.0, 1.0)
    xn_ref[0:HALO, :] = (_rmsnorm(hp_ref[...], nw) * keep_prev).astype(BF16)
    xn_ref[HALO:HALO + T, :] = _rmsnorm(x, nw).astype(BF16)
    xn_ref[HALO + T:, :] = (_rmsnorm(hn_ref[...], nw) * keep_next).astype(BF16)
    xe = xn_ref[...]
    rows = T + 2 * HALO

    def conv(u, cw, cb):
        up = pltpu.roll(u, 1, 0)[HALO:HALO + T]
        un = pltpu.roll(u, rows - 1, 0)[HALO:HALO + T]
        return up * cw[0:1] + u[HALO:HALO + T] * cw[1:2] + un * cw[2:3] + cb

    def up(c):
        cs = slice(c * C, (c + 1) * C)
        return (jnp.dot(xe, wua_ref[:, cs], preferred_element_type=F32),
                jnp.dot(xe, wub_ref[:, cs], preferred_element_type=F32))

    def act(c, ua, ub):
        cs = slice(c * C, (c + 1) * C)
        gs = slice(D_FF + c * C, D_FF + (c + 1) * C)
        a = conv(ua, cw_ref[:, cs], cb_ref[:, cs])
        b = conv(ub, cw_ref[:, gs], cb_ref[:, gs])
        return (a * _sigmoid(a) * b).astype(BF16)

    n_chunks = D_FF // C
    u = up(0)
    for c in range(n_chunks):
        u_next = up(c + 1) if c + 1 < n_chunks else None
        z_ref[:, c * C:(c + 1) * C] = act(c, *u)
        u = u_next
    ffn = jnp.dot(z_ref[...], wd_ref[...], preferred_element_type=F32)
    out_ref[...] = _rmsnorm(x + ffn, nfw_ref[...])


def _conv_ffn(h1, seq_len, nw, wua, wub, cw, cb, wd, nfw):
    n_tok = h1.shape[0]
    T, HALO = TOKEN_TILE, FFN_HALO
    tiles_per_seq = seq_len // T
    r = T // HALO
    n_hblk = n_tok // HALO
    full = lambda a: _resident(a.shape)
    return pl.pallas_call(
        functools.partial(_conv_ffn_kernel, tiles_per_seq=tiles_per_seq),
        grid=(n_tok // T,),
        in_specs=[
            pl.BlockSpec((HALO, D_MODEL), lambda i: (jnp.maximum(i * r - 1, 0), 0)),
            pl.BlockSpec((T, D_MODEL), lambda i: (i, 0)),
            pl.BlockSpec((HALO, D_MODEL), lambda i: (jnp.minimum((i + 1) * r, n_hblk - 1), 0)),
            full(nw), full(wua), full(wub), full(cw), full(cb), full(wd), full(nfw),
        ],
        out_specs=pl.BlockSpec((T, D_MODEL), lambda i: (i, 0)),
        out_shape=jax.ShapeDtypeStruct((n_tok, D_MODEL), F32),
        scratch_shapes=[pltpu.VMEM((T + 2 * HALO, D_MODEL), BF16), pltpu.VMEM((T, D_FF), BF16)],
        compiler_params=pltpu.CompilerParams(
            dimension_semantics=("arbitrary",), vmem_limit_bytes=VMEM_LIMIT),
        name="conv_ffn",
    )(h1, h1, h1, nw, wua, wub, cw, cb, wd, nfw)


def _rope_tables(seq_len):
    half = ROT_DIM // 2
    inv = ROPE_THETA ** (-jnp.arange(half, dtype=F32) / half)
    ang = jnp.arange(seq_len, dtype=F32)[:, None] * inv[None, :]
    cos, sin = jnp.cos(ang), jnp.sin(ang)
    pad = jnp.zeros((seq_len, A_HEAD_DIM - ROT_DIM), F32)
    zero = jnp.zeros_like(sin)
    cos_h = jnp.concatenate([cos, cos, pad + 1.0], axis=1)
    sa_h = jnp.concatenate([-sin, zero, pad], axis=1)
    sb_h = jnp.concatenate([zero, sin, pad], axis=1)
    two = lambda t: jnp.concatenate([t, t], axis=1)
    return two(cos_h), two(sa_h), two(sb_h)


def _encoder(x, p):
    n_seq, seq_len, _ = x.shape
    x2 = x.reshape(n_seq * seq_len, D_MODEL)
    cos_t, sa_t, sb_t = _rope_tables(seq_len)
    k_tok, t_feat, gate_rows, aqkv = _in_proj(x2, seq_len, p["norm1_w"], p["w_mk"], p["w_mt"], p["w_gate_t"],
                                              p["w_a"], p["gate_bias"], cos_t, sa_t, sb_t)
    gs, uc = _gate_scan(gate_rows)
    hf, hb = _mlstm(k_tok, t_feat, gs, uc, n_seq, seq_len)
    h1 = _mixer_out(x2, seq_len, p["norm1_w"], hf, hb, t_feat, p["mh_norm_w"], aqkv, p["sink"],
                    p["w_branch_gate"], p["w_proj_m"], p["w_proj_a"], p["w_out"])
    y = _conv_ffn(h1, seq_len, p["norm2_w"], p["w_up_a"], p["w_up_b"], p["conv_w"], p["conv_b"],
                  p["w_down"], p["norm_f_w"])
    return y.reshape(n_seq, seq_len, D_MODEL)


def _prepare_params(norm1_w, w_in, i_bias, f_bias, mh_norm_w, attn_sink, w_proj_m, w_proj_a,
                    w_out, norm2_w, w_up, conv_w, conv_b, w_down, norm_f_w):
    w_in = w_in[0]
    W = M_WIDTH
    m_end = 4 * W
    g_end = m_end + 4 * M_HEADS
    a_end = g_end + A_WIDTH + 2 * A_KV_WIDTH
    w_mt = jnp.concatenate([w_in[:, 0:W], w_in[:, 2 * W:3 * W], w_in[:, 3 * W:4 * W]], axis=1).T
    gate_bias = jnp.concatenate([i_bias[0].reshape(-1), f_bias[0].reshape(-1)])
    return {
        "norm1_w": norm1_w[0].reshape(1, D_MODEL),
        "w_mk": w_in[:, W:2 * W].astype(BF16),
        "w_mt": w_mt.astype(BF16),
        "w_gate_t": w_in[:, m_end:g_end].T.astype(BF16),
        "gate_bias": jnp.broadcast_to(gate_bias[:, None], (4 * M_HEADS, LANES)),
        "w_a": w_in[:, g_end:a_end].astype(BF16),
        "w_branch_gate": w_in[:, a_end:].astype(BF16),
        "mh_norm_w": jnp.broadcast_to(mh_norm_w[0][:, None], (M_WIDTH, LANES)),
        "sink": jnp.broadcast_to(attn_sink[0][:, None], (A_HEADS, LANES)),
        "w_proj_m": w_proj_m[0].astype(BF16),
        "w_proj_a": w_proj_a[0].astype(BF16),
        "w_out": w_out[0].astype(BF16),
        "norm2_w": norm2_w[0].reshape(1, D_MODEL),
        "w_up_a": w_up[0][:, :D_FF].astype(BF16),
        "w_up_b": w_up[0][:, D_FF:].astype(BF16),
        "conv_w": conv_w[0],
        "conv_b": conv_b[0].reshape(1, 2 * D_FF),
        "w_down": w_down[0].astype(BF16),
        "norm_f_w": norm_f_w.reshape(1, D_MODEL),
    }


def kernel(x_prompt, x_sample, norm1_w, w_in, i_bias, f_bias, mh_norm_w, attn_sink, w_proj_m, w_proj_a,
           w_out, norm2_w, w_up, conv_w, conv_b, w_down, norm_f_w):
    p = _prepare_params(norm1_w, w_in, i_bias, f_bias, mh_norm_w, attn_sink, w_proj_m, w_proj_a,
                        w_out, norm2_w, w_up, conv_w, conv_b, w_down, norm_f_w)
    return (_encoder(x_prompt, p), _encoder(x_sample, p))
```

```python
import functools

import jax
import jax.numpy as jnp
from jax import lax
from jax.experimental import pallas as pl
from jax.experimental.pallas import tpu as pltpu

F32 = jnp.float32
BF16 = jnp.bfloat16

D_MODEL = 1024
M_HEADS = 4
M_HEAD_DIM = 128
M_WIDTH = M_HEADS * M_HEAD_DIM
M_CHUNK = 128
A_HEADS = 8
A_KV_HEADS = 4
A_HEAD_DIM = 64
A_WIDTH = A_HEADS * A_HEAD_DIM
A_KV_WIDTH = A_KV_HEADS * A_HEAD_DIM
WINDOW = 128
A_BLOCK = 128
ROT_DIM = A_HEAD_DIM // 4
ROPE_THETA = 500000.0
D_FF = 2816
EPS = 1e-6
LOG2E = 1.4426950408889634

LANES = 128
IN_TILE = 1024
TOKEN_TILE = 512
FFN_TILE = 1024
FFN_HALO = 16
FFN_CHUNK = 256
M_AUG = 16
GS_ROWS = 5 * 2 * M_HEADS
SCAN_CHUNKS = 32
MLSTM_GROUP = 4
VMEM_LIMIT = 56 * 1024 * 1024


def _resident(shape):
    return pl.BlockSpec(shape, lambda *_: (0,) * len(shape), pipeline_mode=pl.Buffered(1))


def _rmsnorm(x, w):
    return x * lax.rsqrt(jnp.mean(x * x, axis=-1, keepdims=True) + EPS) * w


def _sigmoid(x):
    return 1.0 / (1.0 + jnp.exp(-x))


def _in_proj_kernel(x_ref, nw_ref, wk_ref, wt_ref, wa_ref, gb_ref, cos_ref, sa_ref, sb_ref,
                    k_out, t_out, g_out, a_out):
    T = x_ref.shape[0]
    xb = _rmsnorm(x_ref[...], nw_ref[...]).astype(BF16)
    a = jnp.dot(xb, wa_ref[...], preferred_element_type=F32)
    k_out[...] = jnp.dot(xb, wk_ref[...], preferred_element_type=F32).astype(BF16)
    tt = lax.dot_general(wt_ref[...], xb, (((1,), (1,)), ((), ())),
                         preferred_element_type=F32)
    t_out[0:M_WIDTH, :] = (tt[0:M_WIDTH] * M_HEAD_DIM ** -0.5).astype(BF16)
    t_out[M_WIDTH:, :] = tt[M_WIDTH:3 * M_WIDTH].astype(BF16)
    g_out[...] = tt[3 * M_WIDTH:] + jnp.concatenate([gb_ref[...]] * (T // LANES), axis=1)
    cos, sa, sb = cos_ref[...], sa_ref[...], sb_ref[...]
    half = ROT_DIM // 2
    lane = lax.broadcasted_iota(jnp.int32, (1, LANES), 1)
    low = lane < A_HEAD_DIM

    def rope(t):
        return t * cos + pltpu.roll(t, LANES - half, 1) * sa + pltpu.roll(t, half, 1) * sb

    def dup_heads(t):
        sw = pltpu.roll(t, A_HEAD_DIM, 1)
        return jnp.where(low, t, sw), jnp.where(low, sw, t)

    q_scale = A_HEAD_DIM ** -0.5 * LOG2E
    for j in range(A_WIDTH // LANES):
        a_out[:, j * LANES:(j + 1) * LANES] = (rope(a[:, j * LANES:(j + 1) * LANES]) * q_scale).astype(BF16)
    k_off, v_off = A_WIDTH, A_WIDTH + A_KV_WIDTH
    for j in range(A_KV_WIDTH // LANES):
        k0, k1 = dup_heads(rope(a[:, k_off + j * LANES:k_off + (j + 1) * LANES]))
        v0, v1 = dup_heads(a[:, v_off + j * LANES:v_off + (j + 1) * LANES])
        ko = A_WIDTH + 2 * j * LANES
        vo = 2 * A_WIDTH + 2 * j * LANES
        a_out[:, ko:ko + LANES] = k0.astype(BF16)
        a_out[:, ko + LANES:ko + 2 * LANES] = k1.astype(BF16)
        a_out[:, vo:vo + LANES] = v0.astype(BF16)
        a_out[:, vo + LANES:vo + 2 * LANES] = v1.astype(BF16)


def _in_proj(x2, seq_len, nw, wk, wt, wa, gb, cos_t, sa_t, sb_t):
    n_tok = x2.shape[0]
    T = IN_TILE
    tiles_per_seq = seq_len // T
    full = lambda a: _resident(a.shape)
    rope_spec = pl.BlockSpec((T, LANES), lambda i: (i % tiles_per_seq, 0))
    n_gate = 4 * M_HEADS
    return pl.pallas_call(
        _in_proj_kernel,
        grid=(n_tok // T,),
        in_specs=[
            pl.BlockSpec((T, D_MODEL), lambda i: (i, 0)),
            full(nw), full(wk), full(wt), full(wa), full(gb),
            rope_spec, rope_spec, rope_spec,
        ],
        out_specs=[
            pl.BlockSpec((T, M_WIDTH), lambda i: (i, 0)),
            pl.BlockSpec((3 * M_WIDTH, T), lambda i: (0, i)),
            pl.BlockSpec((n_gate, T), lambda i: (0, i)),
            pl.BlockSpec((T, 3 * A_WIDTH), lambda i: (i, 0)),
        ],
        out_shape=[
            jax.ShapeDtypeStruct((n_tok, M_WIDTH), BF16),
            jax.ShapeDtypeStruct((3 * M_WIDTH, n_tok), BF16),
            jax.ShapeDtypeStruct((n_gate, n_tok), F32),
            jax.ShapeDtypeStruct((n_tok, 3 * A_WIDTH), BF16),
        ],
        compiler_params=pltpu.CompilerParams(
            dimension_semantics=("arbitrary",), vmem_limit_bytes=VMEM_LIMIT),
        name="in_proj",
    )(x2, nw, wk, wt, wa, gb, cos_t, sa_t, sb_t)


def _log_sigmoid(x):
    return jnp.minimum(x, 0.0) - jnp.log(1.0 + jnp.exp(-jnp.abs(x)))


def _scan_lanes(x, op, fill, reverse):
    n = x.shape[1]
    lane = lax.broadcasted_iota(jnp.int32, x.shape, 1)
    sh = 1
    while sh < n:
        if reverse:
            x = op(x, jnp.where(lane < n - sh, pltpu.roll(x, n - sh, 1), fill))
        else:
            x = op(x, jnp.where(lane >= sh, pltpu.roll(x, sh, 1), fill))
        sh *= 2
    return x


def _gate_scan_kernel(g_ref, gs_out, uc_out):
    L, R = M_CHUNK, 2 * M_HEADS
    n_chunk = g_ref.shape[1] // L
    ig = jnp.concatenate([g_ref[0:R, j * L:(j + 1) * L] for j in range(n_chunk)], axis=0)
    lf = _log_sigmoid(jnp.concatenate([g_ref[R:2 * R, j * L:(j + 1) * L] for j in range(n_chunk)], axis=0))
    rows = lax.broadcasted_iota(jnp.int32, ig.shape, 0)
    is_fwd = (rows & (R - 1)) < M_HEADS
    cum = _scan_lanes(lf, jnp.add, 0.0, False)
    tot = jnp.broadcast_to(cum[:, L - 1:L], cum.shape)
    b = jnp.where(is_fwd, cum, tot - cum + lf)
    u = ig - b
    pmax = _scan_lanes(u, jnp.maximum, -jnp.inf, False)
    cmax = jnp.where(is_fwd, pmax, _scan_lanes(u, jnp.maximum, -jnp.inf, True))
    umax = jnp.broadcast_to(pmax[:, L - 1:L], pmax.shape)
    wgt = jnp.exp(u - umax)
    mloc = tot + umax
    pad = jnp.zeros((L - R, L), F32)
    for j in range(n_chunk):
        rs = slice(j * R, (j + 1) * R)
        gs_out[j * GS_ROWS:(j + 1) * GS_ROWS, :] = jnp.concatenate(
            [b[rs], cmax[rs], wgt[rs], tot[rs], mloc[rs]], axis=0)
        uc_out[j * L:(j + 1) * L, :] = jnp.concatenate([u[rs], pad], axis=0).T


def _gate_scan(gt):
    n_tok = gt.shape[1]
    L = M_CHUNK
    blk = SCAN_CHUNKS * L
    return pl.pallas_call(
        _gate_scan_kernel,
        grid=(n_tok // blk,),
        in_specs=[pl.BlockSpec((gt.shape[0], blk), lambda i: (0, i))],
        out_specs=[pl.BlockSpec((SCAN_CHUNKS * GS_ROWS, LANES), lambda i: (i, 0)),
                   pl.BlockSpec((blk, LANES), lambda i: (i, 0))],
        out_shape=[jax.ShapeDtypeStruct((n_tok // L * GS_ROWS, LANES), F32),
                   jax.ShapeDtypeStruct((n_tok, LANES), F32)],
        compiler_params=pltpu.CompilerParams(
            dimension_semantics=("arbitrary",), vmem_limit_bytes=VMEM_LIMIT),
        name="gate_scan",
    )(gt)


def _mlstm_kernel(kf_ref, qf_ref, vf_ref, gsf_ref, ucf_ref, kb_ref, qb_ref, vb_ref, gsb_ref, ucb_ref,
                  hf_out, hb_out, cn_ref, m_ref):
    L, DH, H, G = M_CHUNK, M_HEAD_DIM, M_HEADS, MLSTM_GROUP
    R = 2 * H

    @pl.when(pl.program_id(1) == 0)
    def _():
        cn_ref[...] = jnp.zeros_like(cn_ref)
        m_ref[...] = jnp.zeros_like(m_ref)

    is_fwd = lax.broadcasted_iota(jnp.int32, (R, L), 0) < H
    si = lax.broadcasted_iota(jnp.int32, (L, L), 0)
    ti = lax.broadcasted_iota(jnp.int32, (L, L), 1)
    tri = (jnp.where(si <= ti, 0.0, -jnp.inf), jnp.where(si >= ti, 0.0, -jnp.inf))
    ones_rows = jnp.where(lax.broadcasted_iota(jnp.int32, (M_AUG, L), 0) == 0, 1.0, 0.0).astype(BF16)
    row = lambda a, r: a[r:r + 1]

    for j in range(G):
        jf, jb = j, G - 1 - j
        gsf = gsf_ref[jf * GS_ROWS:(jf + 1) * GS_ROWS, :]
        gsb = gsb_ref[jb * GS_ROWS:(jb + 1) * GS_ROWS, :]
        b, cmax, wgt, tot, mloc = (
            jnp.where(is_fwd, gsf[n * R:(n + 1) * R], gsb[n * R:(n + 1) * R]) for n in range(5))
        m_prev = m_ref[...]
        g = jnp.maximum(m_prev, cmax)
        rho = jnp.exp(cmax - g)
        omega = jnp.exp(m_prev - g)
        thr = jnp.exp(-(b + g))
        m_new = jnp.maximum(tot + m_prev, mloc)
        s_prev = jnp.exp(tot + m_prev - m_new)
        s_loc = jnp.exp(mloc - m_new)
        m_ref[...] = m_new

        def operands(r):
            d, h = divmod(r, H)
            k_ref, q_ref, v_ref, uc_ref, jj = ((kf_ref, qf_ref, vf_ref, ucf_ref, jf) if d == 0
                                               else (kb_ref, qb_ref, vb_ref, ucb_ref, jb))
            hs, ts = slice(h * DH, (h + 1) * DH), slice(jj * L, (jj + 1) * L)
            v_aug = jnp.concatenate([v_ref[hs, ts], ones_rows], axis=0)
            return k_ref[ts, hs], q_ref[hs, ts], v_aug, uc_ref[ts, r:r + 1]

        ops = [operands(r) for r in range(R)]
        s_t = [jnp.dot(k, q_t, preferred_element_type=F32) for k, q_t, _, _ in ops]
        vw = [(v_aug.astype(F32) * row(wgt, r)).astype(BF16) for r, (_, _, v_aug, _) in enumerate(ops)]
        cn_loc = [jnp.dot(vw[r], ops[r][0], preferred_element_type=F32) for r in range(R)]
        rhs = []
        for r, (k, q_t, v_aug, u_col) in enumerate(ops):
            decay = jnp.exp(u_col + (tri[r // H] - row(cmax, r)))
            p_t = (s_t[r] * decay * row(rho, r)).astype(BF16)
            q_w = (q_t.astype(F32) * row(omega, r)).astype(BF16)
            rhs.append(jnp.concatenate([p_t, q_w], axis=0))
        for r, (k, q_t, v_aug, u_col) in enumerate(ops):
            d, h = divmod(r, H)
            cn = cn_ref[r]
            lhs = jnp.concatenate([v_aug, cn.astype(BF16)], axis=1)
            out = jnp.dot(lhs, rhs[r], preferred_element_type=F32)
            rec = 1.0 / jnp.maximum(jnp.abs(out[DH:DH + 1]), row(thr, r))
            h_out, jj = (hf_out, jf) if d == 0 else (hb_out, jb)
            h_out[h * DH:(h + 1) * DH, jj * L:(jj + 1) * L] = out[:DH] * rec
            cn_ref[r] = row(s_prev, r) * cn + row(s_loc, r) * cn_loc[r]


def _mlstm(k_tok, t_feat, gs, uc, n_seq, seq_len):
    n_tok = k_tok.shape[0]
    L, G = M_CHUNK, MLSTM_GROUP
    nb = seq_len // (L * G)
    fwd = lambda s, c: s * nb + c
    bwd = lambda s, c: s * nb + nb - 1 - c
    specs = lambda at: [
        pl.BlockSpec((G * L, M_WIDTH), lambda s, c: (at(s, c), 0)),
        pl.BlockSpec((M_WIDTH, G * L), lambda s, c: (0, at(s, c))),
        pl.BlockSpec((M_WIDTH, G * L), lambda s, c: (1, at(s, c))),
        pl.BlockSpec((G * GS_ROWS, LANES), lambda s, c: (at(s, c), 0)),
        pl.BlockSpec((G * L, LANES), lambda s, c: (at(s, c), 0)),
    ]
    out_spec = lambda at: pl.BlockSpec((M_WIDTH, G * L), lambda s, c: (0, at(s, c)))
    return pl.pallas_call(
        _mlstm_kernel,
        grid=(n_seq, nb),
        in_specs=specs(fwd) + specs(bwd),
        out_specs=[out_spec(fwd), out_spec(bwd)],
        out_shape=[jax.ShapeDtypeStruct((M_WIDTH, n_tok), F32)] * 2,
        scratch_shapes=[
            pltpu.VMEM((2 * M_HEADS, M_HEAD_DIM + M_AUG, M_HEAD_DIM), F32),
            pltpu.VMEM((2 * M_HEADS, LANES), F32),
        ],
        compiler_params=pltpu.CompilerParams(
            dimension_semantics=("arbitrary", "arbitrary"), vmem_limit_bytes=VMEM_LIMIT),
        name="mlstm",
    )(k_tok, t_feat, t_feat, gs, uc, k_tok, t_feat, t_feat, gs, uc)


def _mixer_out_kernel(x_ref, nw_ref, hf_ref, hb_ref, mo_ref, mhw_ref, q_ref,
                      kp_ref, kc_ref, kn_ref, vp_ref, vc_ref, vn_ref, sink_ref,
                      wg_ref, wpm_ref, wpa_ref, wo_ref, out_ref, *, tiles_per_seq):
    T, L = TOKEN_TILE, A_BLOCK
    t_in_seq = pl.program_id(0) % tiles_per_seq

    k_ext = jnp.concatenate([kp_ref[...], kc_ref[...], kn_ref[...]], axis=0)
    v_ext = jnp.concatenate([vp_ref[...], vc_ref[...], vn_ref[...]], axis=0)
    lane = lax.broadcasted_iota(jnp.int32, (1, LANES), 1)
    low = lane < A_HEAD_DIM
    n_blk = T // L
    W3 = 3 * L
    zero = jnp.zeros((), BF16)

    def scores(g):
        gs = slice(g * LANES, (g + 1) * LANES)
        kd = k_ext[:, gs]
        k_lo, k_hi = jnp.where(low, kd, zero), jnp.where(low, zero, kd)
        rows = []
        for i in range(n_blk):
            ks = slice(i * L, (i + 3) * L)
            kbd = jnp.concatenate([k_lo[ks], k_hi[ks]], axis=0)
            rows.append(lax.dot_general(q_ref[i * L:(i + 1) * L, gs], kbd, (((1,), (1,)), ((), ())),
                                        preferred_element_type=F32))
        return jnp.concatenate(rows, axis=0)

    def softmax(g, i, s):
        rs = slice(i * L, (i + 1) * L)
        p_parts, invs = [], []
        for sub in range(2):
            sink = sinks2[2 * g + sub:2 * g + sub + 1, 0:1]
            c0 = s[rs, sub * W3:sub * W3 + L] + bias_prev[i]
            c1 = s[rs, sub * W3 + L:sub * W3 + 2 * L]
            c2 = s[rs, sub * W3 + 2 * L:(sub + 1) * W3] + bias_next[i]
            m = jnp.maximum(jnp.max(jnp.maximum(jnp.maximum(c0, c1), c2), axis=-1, keepdims=True), sink)
            p0, p1, p2 = jnp.exp2(c0 - m), jnp.exp2(c1 - m), jnp.exp2(c2 - m)
            denom = jnp.sum(p0 + p1 + p2, axis=-1, keepdims=True) + jnp.exp2(sink - m)
            invs.append(1.0 / denom)
            p_parts += [p0.astype(BF16), p1.astype(BF16), p2.astype(BF16)]
        return jnp.concatenate(p_parts, axis=1), jnp.where(low, invs[0], invs[1])

    def attend(g, s):
        gs = slice(g * LANES, (g + 1) * LANES)
        vd = v_ext[:, gs]
        v_lo, v_hi = jnp.where(low, vd, zero), jnp.where(low, zero, vd)
        outs = []
        for i in range(n_blk):
            p, inv = softmax(g, i, s)
            ks = slice(i * L, (i + 3) * L)
            vbd = jnp.concatenate([v_lo[ks], v_hi[ks]], axis=0)
            outs.append((jnp.dot(p, vbd, preferred_element_type=F32) * inv).astype(BF16))
        return jnp.concatenate(outs, axis=0)

    s = scores(0)

    x = x_ref[...]
    xb = _rmsnorm(x, nw_ref[...]).astype(BF16)
    qi = lax.broadcasted_iota(jnp.int32, (L, L), 0)
    kj = lax.broadcasted_iota(jnp.int32, (L, L), 1)
    tri_prev = jnp.where(kj >= qi, 0.0, -jnp.inf)
    tri_next = jnp.where(kj <= qi, 0.0, -jnp.inf)
    edge_prev = jnp.where((kj >= qi) & (t_in_seq > 0), 0.0, -jnp.inf)
    edge_next = jnp.where((kj <= qi) & (t_in_seq < tiles_per_seq - 1), 0.0, -jnp.inf)
    bias_prev = [edge_prev] + [tri_prev] * (n_blk - 1)
    bias_next = [tri_next] * (n_blk - 1) + [edge_next]
    sinks2 = sink_ref[...] * LOG2E

    hsum = hf_ref[...] + hb_ref[...]
    parts = []
    for h in range(M_HEADS):
        hs = slice(h * M_HEAD_DIM, (h + 1) * M_HEAD_DIM)
        hh = hsum[hs]
        hh = hh - jnp.mean(hh, axis=0, keepdims=True)
        hh = hh * lax.rsqrt(jnp.mean(hh * hh, axis=0, keepdims=True) + EPS)
        w = jnp.concatenate([mhw_ref[hs, :]] * (T // LANES), axis=1)
        parts.append((_sigmoid(mo_ref[hs, :].astype(F32)) * (hh * w)).astype(BF16))
    hg_t = jnp.concatenate(parts, axis=0)
    y_m = lax.dot_general(hg_t, wpm_ref[...], (((0,), (0,)), ((), ())), preferred_element_type=F32)

    o_cols = []
    s_next = scores(1)
    gate_m = jnp.dot(xb, wg_ref[:, :D_MODEL], preferred_element_type=F32)
    o_cols.append(attend(0, s))
    s, s_next = s_next, scores(2)
    gate_a = jnp.dot(xb, wg_ref[:, D_MODEL:], preferred_element_type=F32)
    o_cols.append(attend(1, s))
    s, s_next = s_next, scores(3)
    o_cols.append(attend(2, s))
    y_m = _sigmoid(gate_m) * y_m
    o_cols.append(attend(3, s_next))
    oa = jnp.concatenate(o_cols, axis=1)
    sig_a = _sigmoid(gate_a)

    half_rows = (slice(0, T // 2), slice(T // 2, T))
    y_a = [jnp.dot(oa[rs], wpa_ref[...], preferred_element_type=F32) for rs in half_rows]
    for rs, ya in zip(half_rows, y_a):
        y = (y_m[rs] + sig_a[rs] * ya).astype(BF16)
        out_ref[rs, :] = x[rs] + jnp.dot(y, wo_ref[...], preferred_element_type=F32)


def _mixer_out(x2, seq_len, nw, hf, hb, t_feat, mhw, aqkv, sink, wg, wpm, wpa, wo):
    n_tok = x2.shape[0]
    T, L = TOKEN_TILE, A_BLOCK
    tiles_per_seq = seq_len // T
    r = T // L
    n_lblk = n_tok // L
    full = lambda a: _resident(a.shape)
    tile = lambda width, j: pl.BlockSpec((T, width), lambda i: (i, j))
    feat = lambda j: pl.BlockSpec((M_WIDTH, T), lambda i: (j, i))
    prev = lambda j: pl.BlockSpec((L, A_WIDTH), lambda i: (jnp.maximum(i * r - 1, 0), j))
    nxt = lambda j: pl.BlockSpec((L, A_WIDTH), lambda i: (jnp.minimum((i + 1) * r, n_lblk - 1), j))
    return pl.pallas_call(
        functools.partial(_mixer_out_kernel, tiles_per_seq=tiles_per_seq),
        grid=(n_tok // T,),
        in_specs=[
            tile(D_MODEL, 0), full(nw),
            feat(0), feat(0), feat(2), full(mhw),
            tile(A_WIDTH, 0),
            prev(1), tile(A_WIDTH, 1), nxt(1),
            prev(2), tile(A_WIDTH, 2), nxt(2),
            full(sink), full(wg), full(wpm), full(wpa), full(wo),
        ],
        out_specs=tile(D_MODEL, 0),
        out_shape=jax.ShapeDtypeStruct((n_tok, D_MODEL), F32),
        compiler_params=pltpu.CompilerParams(
            dimension_semantics=("arbitrary",), vmem_limit_bytes=VMEM_LIMIT),
        name="mixer_out",
    )(x2, nw, hf, hb, t_feat, mhw, aqkv, aqkv, aqkv, aqkv, aqkv, aqkv, aqkv,
      sink, wg, wpm, wpa, wo)


def _conv_ffn_kernel(hp_ref, hc_ref, hn_ref, nw_ref, wua_ref, wub_ref, cw_ref, cb_ref, wd_ref,
                     nfw_ref, out_ref, xn_ref, z_ref, *, tiles_per_seq):
    T, HALO, C = FFN_TILE, FFN_HALO, FFN_CHUNK
    t_in_seq = pl.program_id(0) % tiles_per_seq
    nw = nw_ref[...]
    x = hc_ref[...]
    keep_prev = jnp.where(t_in_seq == 0, 0.0, 1.0)
    keep_next = jnp.where(t_in_seq == tiles_per_seq - 1, 0.0, 1.0)
    xn_ref[0:HALO, :] = (_rmsnorm(hp_ref[...], nw) * keep_prev).astype(BF16)
    xn_ref[HALO:HALO + T, :] = _rmsnorm(x, nw).astype(BF16)
    xn_ref[HALO + T:, :] = (_rmsnorm(hn_ref[...], nw) * keep_next).astype(BF16)
    xe = xn_ref[...]
    rows = T + 2 * HALO

    def conv(u, cw, cb):
        up = pltpu.roll(u, 1, 0)[HALO:HALO + T]
        un = pltpu.roll(u, rows - 1, 0)[HALO:HALO + T]
        return up * cw[0:1] + u[HALO:HALO + T] * cw[1:2] + un * cw[2:3] + cb

    def up(c):
        cs = slice(c * C, (c + 1) * C)
        return (jnp.dot(xe, wua_ref[:, cs], preferred_element_type=F32),
                jnp.dot(xe, wub_ref[:, cs], preferred_element_type=F32))

    def act(c, ua, ub):
        cs = slice(c * C, (c + 1) * C)
        gs = slice(D_FF + c * C, D_FF + (c + 1) * C)
        a = conv(ua, cw_ref[:, cs], cb_ref[:, cs])
        b = conv(ub, cw_ref[:, gs], cb_ref[:, gs])
        return (a * _sigmoid(a) * b).astype(BF16)

    n_chunks = D_FF // C
    u = up(0)
    for c in range(n_chunks):
        u_next = up(c + 1) if c + 1 < n_chunks else None
        z_ref[:, c * C:(c + 1) * C] = act(c, *u)
        u = u_next
    ffn = jnp.dot(z_ref[...], wd_ref[...], preferred_element_type=F32)
    out_ref[...] = _rmsnorm(x + ffn, nfw_ref[...])


def _conv_ffn(h1, seq_len, nw, wua, wub, cw, cb, wd, nfw):
    n_tok = h1.shape[0]
    T, HALO = FFN_TILE, FFN_HALO
    tiles_per_seq = seq_len // T
    r = T // HALO
    n_hblk = n_tok // HALO
    full = lambda a: _resident(a.shape)
    return pl.pallas_call(
        functools.partial(_conv_ffn_kernel, tiles_per_seq=tiles_per_seq),
        grid=(n_tok // T,),
        in_specs=[
            pl.BlockSpec((HALO, D_MODEL), lambda i: (jnp.maximum(i * r - 1, 0), 0)),
            pl.BlockSpec((T, D_MODEL), lambda i: (i, 0)),
            pl.BlockSpec((HALO, D_MODEL), lambda i: (jnp.minimum((i + 1) * r, n_hblk - 1), 0)),
            full(nw), full(wua), full(wub), full(cw), full(cb), full(wd), full(nfw),
        ],
        out_specs=pl.BlockSpec((T, D_MODEL), lambda i: (i, 0)),
        out_shape=jax.ShapeDtypeStruct((n_tok, D_MODEL), F32),
        scratch_shapes=[pltpu.VMEM((T + 2 * HALO, D_MODEL), BF16), pltpu.VMEM((T, D_FF), BF16)],
        compiler_params=pltpu.CompilerParams(
            dimension_semantics=("arbitrary",), vmem_limit_bytes=VMEM_LIMIT),
        name="conv_ffn",
    )(h1, h1, h1, nw, wua, wub, cw, cb, wd, nfw)


def _rope_tables(seq_len):
    half = ROT_DIM // 2
    inv = ROPE_THETA ** (-jnp.arange(half, dtype=F32) / half)
    ang = jnp.arange(seq_len, dtype=F32)[:, None] * inv[None, :]
    cos, sin = lax.optimization_barrier((jnp.cos(ang), jnp.sin(ang)))
    pad = jnp.zeros((seq_len, A_HEAD_DIM - ROT_DIM), F32)
    zero = jnp.zeros_like(sin)
    cos_h = jnp.concatenate([cos, cos, pad + 1.0], axis=1)
    sa_h = jnp.concatenate([-sin, zero, pad], axis=1)
    sb_h = jnp.concatenate([zero, sin, pad], axis=1)
    two = lambda t: jnp.concatenate([t, t], axis=1)
    return two(cos_h), two(sa_h), two(sb_h)


def _encoder(x, p, rope):
    n_seq, seq_len, _ = x.shape
    assert seq_len % max(IN_TILE, TOKEN_TILE, FFN_TILE, MLSTM_GROUP * M_CHUNK) == 0, seq_len
    assert (n_seq * seq_len) % (SCAN_CHUNKS * M_CHUNK) == 0, (n_seq, seq_len)
    x2 = x.reshape(n_seq * seq_len, D_MODEL)
    k_tok, t_feat, gate_rows, aqkv = _in_proj(x2, seq_len, p["norm1_w"], p["w_mk"], p["w_mt"],
                                              p["w_a"], p["gate_bias"], *rope)
    gs, uc = _gate_scan(gate_rows)
    hf, hb = _mlstm(k_tok, t_feat, gs, uc, n_seq, seq_len)
    h1 = _mixer_out(x2, seq_len, p["norm1_w"], hf, hb, t_feat, p["mh_norm_w"], aqkv, p["sink"],
                    p["w_branch_gate"], p["w_proj_m"], p["w_proj_a"], p["w_out"])
    y = _conv_ffn(h1, seq_len, p["norm2_w"], p["w_up_a"], p["w_up_b"], p["conv_w"], p["conv_b"],
                  p["w_down"], p["norm_f_w"])
    return y.reshape(n_seq, seq_len, D_MODEL)


def _prepare_params(norm1_w, w_in, i_bias, f_bias, mh_norm_w, attn_sink, w_proj_m, w_proj_a,
                    w_out, norm2_w, w_up, conv_w, conv_b, w_down, norm_f_w):
    w_in = w_in[0]
    W = M_WIDTH
    m_end = 4 * W
    g_end = m_end + 4 * M_HEADS
    a_end = g_end + A_WIDTH + 2 * A_KV_WIDTH
    w_mt = jnp.concatenate([w_in[:, 0:W], w_in[:, 2 * W:4 * W], w_in[:, m_end:g_end]],
                           axis=1).astype(BF16).T
    gate_bias = jnp.concatenate([i_bias[0].reshape(-1), f_bias[0].reshape(-1)])
    return {
        "norm1_w": norm1_w[0].reshape(1, D_MODEL),
        "w_mk": w_in[:, W:2 * W].astype(BF16),
        "w_mt": w_mt,
        "gate_bias": jnp.broadcast_to(gate_bias[:, None], (4 * M_HEADS, LANES)),
        "w_a": w_in[:, g_end:a_end].astype(BF16),
        "w_branch_gate": w_in[:, a_end:].astype(BF16),
        "mh_norm_w": jnp.broadcast_to(mh_norm_w[0][:, None], (M_WIDTH, LANES)),
        "sink": jnp.broadcast_to(attn_sink[0][:, None], (A_HEADS, LANES)),
        "w_proj_m": w_proj_m[0].astype(BF16),
        "w_proj_a": w_proj_a[0].astype(BF16),
        "w_out": w_out[0].astype(BF16),
        "norm2_w": norm2_w[0].reshape(1, D_MODEL),
        "w_up_a": w_up[0][:, :D_FF].astype(BF16),
        "w_up_b": w_up[0][:, D_FF:].astype(BF16),
        "conv_w": conv_w[0],
        "conv_b": conv_b[0].reshape(1, 2 * D_FF),
        "w_down": w_down[0].astype(BF16),
        "norm_f_w": norm_f_w.reshape(1, D_MODEL),
    }


def kernel(x_prompt, x_sample, norm1_w, w_in, i_bias, f_bias, mh_norm_w, attn_sink, w_proj_m, w_proj_a,
           w_out, norm2_w, w_up, conv_w, conv_b, w_down, norm_f_w):
    p = _prepare_params(norm1_w, w_in, i_bias, f_bias, mh_norm_w, attn_sink, w_proj_m, w_proj_a,
                        w_out, norm2_w, w_up, conv_w, conv_b, w_down, norm_f_w)
    rope = _rope_tables(max(x_prompt.shape[1], x_sample.shape[1]))
    return (_encoder(x_prompt, p, rope), _encoder(x_sample, p, rope))
```

```python
import functools

import jax
import jax.numpy as jnp
from jax import lax
from jax.experimental import pallas as pl
from jax.experimental.pallas import tpu as pltpu

F32 = jnp.float32
BF16 = jnp.bfloat16

D_MODEL = 1024
M_HEADS = 4
M_HEAD_DIM = 128
M_WIDTH = M_HEADS * M_HEAD_DIM
M_CHUNK = 128
A_HEADS = 8
A_KV_HEADS = 4
A_HEAD_DIM = 64
A_WIDTH = A_HEADS * A_HEAD_DIM
A_KV_WIDTH = A_KV_HEADS * A_HEAD_DIM
WINDOW = 128
A_BLOCK = 128
ROT_DIM = A_HEAD_DIM // 4
ROPE_THETA = 500000.0
D_FF = 2816
EPS = 1e-6
LOG2E = 1.4426950408889634

LANES = 128
IN_TILE = 1024
TOKEN_TILE = 512
FFN_TILE = 1024
FFN_HALO = 16
FFN_CHUNK = 256
M_AUG = 16
GS_ROWS = 5 * 2 * M_HEADS
SCAN_CHUNKS = 32
MLSTM_GROUP = 4
VMEM_LIMIT = 56 * 1024 * 1024


def _resident(shape):
    return pl.BlockSpec(shape, lambda *_: (0,) * len(shape), pipeline_mode=pl.Buffered(1))


def _rmsnorm(x, w):
    return x * lax.rsqrt(jnp.mean(x * x, axis=-1, keepdims=True) + EPS) * w


def _sigmoid(x):
    return 1.0 / (1.0 + jnp.exp(-x))


def _in_proj_kernel(x_ref, nw_ref, wk_ref, wt_ref, wa_ref, gb_ref, cs_ref,
                    k_out, t_out, g_out, a_out):
    T = x_ref.shape[0]
    xb = _rmsnorm(x_ref[...], nw_ref[...]).astype(BF16)
    a = jnp.dot(xb, wa_ref[...], preferred_element_type=F32)
    k_out[...] = jnp.dot(xb, wk_ref[...], preferred_element_type=F32).astype(BF16)
    tt = lax.dot_general(wt_ref[...], xb, (((1,), (1,)), ((), ())),
                         preferred_element_type=F32)
    t_out[0:M_WIDTH, :] = (tt[0:M_WIDTH] * M_HEAD_DIM ** -0.5).astype(BF16)
    t_out[M_WIDTH:, :] = tt[M_WIDTH:3 * M_WIDTH].astype(BF16)
    g_out[...] = tt[3 * M_WIDTH:] + jnp.concatenate([gb_ref[...]] * (T // LANES), axis=1)

    half = ROT_DIM // 2
    lane = lax.broadcasted_iota(jnp.int32, (1, LANES), 1)
    low = lane < A_HEAD_DIM
    hl = lane & (A_HEAD_DIM - 1)
    first, second = hl < half, hl < ROT_DIM
    cs = cs_ref[...]
    cs2 = jnp.where(low, cs, pltpu.roll(cs, A_HEAD_DIM, 1))
    cos = jnp.where(first, cs2, jnp.where(second, pltpu.roll(cs2, half, 1), 1.0))
    sin = jnp.where(first, -pltpu.roll(cs2, LANES - half, 1), jnp.where(second, cs2, 0.0))

    def rope(t):
        partner = jnp.where(first, pltpu.roll(t, LANES - half, 1), pltpu.roll(t, half, 1))
        return t * cos + partner * sin

    def dup_heads(t):
        sw = pltpu.roll(t, A_HEAD_DIM, 1)
        return jnp.where(low, t, sw), jnp.where(low, sw, t)

    q_scale = A_HEAD_DIM ** -0.5 * LOG2E
    for j in range(A_WIDTH // LANES):
        a_out[:, j * LANES:(j + 1) * LANES] = (rope(a[:, j * LANES:(j + 1) * LANES]) * q_scale).astype(BF16)
    k_off, v_off = A_WIDTH, A_WIDTH + A_KV_WIDTH
    for j in range(A_KV_WIDTH // LANES):
        k0, k1 = dup_heads(rope(a[:, k_off + j * LANES:k_off + (j + 1) * LANES]))
        v0, v1 = dup_heads(a[:, v_off + j * LANES:v_off + (j + 1) * LANES])
        ko = A_WIDTH + 2 * j * LANES
        vo = 2 * A_WIDTH + 2 * j * LANES
        a_out[:, ko:ko + LANES] = k0.astype(BF16)
        a_out[:, ko + LANES:ko + 2 * LANES] = k1.astype(BF16)
        a_out[:, vo:vo + LANES] = v0.astype(BF16)
        a_out[:, vo + LANES:vo + 2 * LANES] = v1.astype(BF16)


def _in_proj(x2, seq_len, nw, wk, wt, wa, gb, cs_t):
    n_tok = x2.shape[0]
    T = IN_TILE
    tiles_per_seq = seq_len // T
    full = lambda a: _resident(a.shape)
    rope_spec = pl.BlockSpec((T, LANES), lambda i: (i % tiles_per_seq, 0))
    n_gate = 4 * M_HEADS
    return pl.pallas_call(
        _in_proj_kernel,
        grid=(n_tok // T,),
        in_specs=[
            pl.BlockSpec((T, D_MODEL), lambda i: (i, 0)),
            full(nw), full(wk), full(wt), full(wa), full(gb),
            rope_spec,
        ],
        out_specs=[
            pl.BlockSpec((T, M_WIDTH), lambda i: (i, 0)),
            pl.BlockSpec((3 * M_WIDTH, T), lambda i: (0, i)),
            pl.BlockSpec((n_gate, T), lambda i: (0, i)),
            pl.BlockSpec((T, 3 * A_WIDTH), lambda i: (i, 0)),
        ],
        out_shape=[
            jax.ShapeDtypeStruct((n_tok, M_WIDTH), BF16),
            jax.ShapeDtypeStruct((3 * M_WIDTH, n_tok), BF16),
            jax.ShapeDtypeStruct((n_gate, n_tok), F32),
            jax.ShapeDtypeStruct((n_tok, 3 * A_WIDTH), BF16),
        ],
        compiler_params=pltpu.CompilerParams(
            dimension_semantics=("arbitrary",), vmem_limit_bytes=VMEM_LIMIT),
        name="in_proj",
    )(x2, nw, wk, wt, wa, gb, cs_t)


def _log_sigmoid(x):
    return jnp.minimum(x, 0.0) - jnp.log(1.0 + jnp.exp(-jnp.abs(x)))


def _scan_lanes(x, op, fill, reverse):
    n = x.shape[1]
    lane = lax.broadcasted_iota(jnp.int32, x.shape, 1)
    sh = 1
    while sh < n:
        if reverse:
            x = op(x, jnp.where(lane < n - sh, pltpu.roll(x, n - sh, 1), fill))
        else:
            x = op(x, jnp.where(lane >= sh, pltpu.roll(x, sh, 1), fill))
        sh *= 2
    return x


def _gate_scan_kernel(g_ref, gs_out, uc_out):
    L, R = M_CHUNK, 2 * M_HEADS
    n_chunk = g_ref.shape[1] // L
    ig = jnp.concatenate([g_ref[0:R, j * L:(j + 1) * L] for j in range(n_chunk)], axis=0)
    lf = _log_sigmoid(jnp.concatenate([g_ref[R:2 * R, j * L:(j + 1) * L] for j in range(n_chunk)], axis=0))
    rows = lax.broadcasted_iota(jnp.int32, ig.shape, 0)
    is_fwd = (rows & (R - 1)) < M_HEADS
    cum = _scan_lanes(lf, jnp.add, 0.0, False)
    tot = jnp.broadcast_to(cum[:, L - 1:L], cum.shape)
    b = jnp.where(is_fwd, cum, tot - cum + lf)
    u = ig - b
    pmax = _scan_lanes(u, jnp.maximum, -jnp.inf, False)
    cmax = jnp.where(is_fwd, pmax, _scan_lanes(u, jnp.maximum, -jnp.inf, True))
    umax = jnp.broadcast_to(pmax[:, L - 1:L], pmax.shape)
    wgt = jnp.exp(u - umax)
    mloc = tot + umax
    pad = jnp.zeros((L - R, L), F32)
    for j in range(n_chunk):
        rs = slice(j * R, (j + 1) * R)
        gs_out[j * GS_ROWS:(j + 1) * GS_ROWS, :] = jnp.concatenate(
            [b[rs], cmax[rs], wgt[rs], tot[rs], mloc[rs]], axis=0)
        uc_out[j * L:(j + 1) * L, :] = jnp.concatenate([u[rs], pad], axis=0).T


def _gate_scan(gt):
    n_tok = gt.shape[1]
    L = M_CHUNK
    blk = SCAN_CHUNKS * L
    return pl.pallas_call(
        _gate_scan_kernel,
        grid=(n_tok // blk,),
        in_specs=[pl.BlockSpec((gt.shape[0], blk), lambda i: (0, i))],
        out_specs=[pl.BlockSpec((SCAN_CHUNKS * GS_ROWS, LANES), lambda i: (i, 0)),
                   pl.BlockSpec((blk, LANES), lambda i: (i, 0))],
        out_shape=[jax.ShapeDtypeStruct((n_tok // L * GS_ROWS, LANES), F32),
                   jax.ShapeDtypeStruct((n_tok, LANES), F32)],
        compiler_params=pltpu.CompilerParams(
            dimension_semantics=("arbitrary",), vmem_limit_bytes=VMEM_LIMIT),
        name="gate_scan",
    )(gt)


def _mlstm_kernel(kf_ref, qf_ref, vf_ref, gsf_ref, ucf_ref, kb_ref, qb_ref, vb_ref, gsb_ref, ucb_ref,
                  hf_out, hb_out, cn_ref, m_ref):
    L, DH, H, G = M_CHUNK, M_HEAD_DIM, M_HEADS, MLSTM_GROUP
    R = 2 * H

    @pl.when(pl.program_id(1) == 0)
    def _():
        cn_ref[...] = jnp.zeros_like(cn_ref)
        m_ref[...] = jnp.zeros_like(m_ref)

    is_fwd = lax.broadcasted_iota(jnp.int32, (R, L), 0) < H
    si = lax.broadcasted_iota(jnp.int32, (L, L), 0)
    ti = lax.broadcasted_iota(jnp.int32, (L, L), 1)
    tri = (jnp.where(si <= ti, 0.0, -jnp.inf), jnp.where(si >= ti, 0.0, -jnp.inf))
    ones_rows = jnp.where(lax.broadcasted_iota(jnp.int32, (M_AUG, L), 0) == 0, 1.0, 0.0).astype(BF16)
    row = lambda a, r: a[r:r + 1]

    for j in range(G):
        jf, jb = j, G - 1 - j
        gsf = gsf_ref[jf * GS_ROWS:(jf + 1) * GS_ROWS, :]
        gsb = gsb_ref[jb * GS_ROWS:(jb + 1) * GS_ROWS, :]
        b, cmax, wgt, tot, mloc = (
            jnp.where(is_fwd, gsf[n * R:(n + 1) * R], gsb[n * R:(n + 1) * R]) for n in range(5))
        m_prev = m_ref[...]
        g = jnp.maximum(m_prev, cmax)
        rho = jnp.exp(cmax - g)
        omega = jnp.exp(m_prev - g)
        thr = jnp.exp(-(b + g))
        m_new = jnp.maximum(tot + m_prev, mloc)
        s_prev = jnp.exp(tot + m_prev - m_new)
        s_loc = jnp.exp(mloc - m_new)
        m_ref[...] = m_new

        def operands(r):
            d, h = divmod(r, H)
            k_ref, q_ref, v_ref, uc_ref, jj = ((kf_ref, qf_ref, vf_ref, ucf_ref, jf) if d == 0
                                               else (kb_ref, qb_ref, vb_ref, ucb_ref, jb))
            hs, ts = slice(h * DH, (h + 1) * DH), slice(jj * L, (jj + 1) * L)
            v_aug = jnp.concatenate([v_ref[hs, ts], ones_rows], axis=0)
            return k_ref[ts, hs], q_ref[hs, ts], v_aug, uc_ref[ts, r:r + 1]

        ops = [operands(r) for r in range(R)]
        s_t = [jnp.dot(k, q_t, preferred_element_type=F32) for k, q_t, _, _ in ops]
        vw = [(v_aug.astype(F32) * row(wgt, r)).astype(BF16) for r, (_, _, v_aug, _) in enumerate(ops)]
        cn_loc = [jnp.dot(vw[r], ops[r][0], preferred_element_type=F32) for r in range(R)]
        rhs = []
        for r, (k, q_t, v_aug, u_col) in enumerate(ops):
            decay = jnp.exp(u_col + (tri[r // H] - row(cmax, r)))
            p_t = (s_t[r] * decay * row(rho, r)).astype(BF16)
            q_w = (q_t.astype(F32) * row(omega, r)).astype(BF16)
            rhs.append(jnp.concatenate([p_t, q_w], axis=0))
        for r, (k, q_t, v_aug, u_col) in enumerate(ops):
            d, h = divmod(r, H)
            cn = cn_ref[r]
            lhs = jnp.concatenate([v_aug, cn.astype(BF16)], axis=1)
            out = jnp.dot(lhs, rhs[r], preferred_element_type=F32)
            rec = 1.0 / jnp.maximum(jnp.abs(out[DH:DH + 1]), row(thr, r))
            h_out, jj = (hf_out, jf) if d == 0 else (hb_out, jb)
            h_out[h * DH:(h + 1) * DH, jj * L:(jj + 1) * L] = out[:DH] * rec
            cn_ref[r] = row(s_prev, r) * cn + row(s_loc, r) * cn_loc[r]


def _mlstm(k_tok, t_feat, gs, uc, n_seq, seq_len):
    n_tok = k_tok.shape[0]
    L, G = M_CHUNK, MLSTM_GROUP
    nb = seq_len // (L * G)
    fwd = lambda s, c: s * nb + c
    bwd = lambda s, c: s * nb + nb - 1 - c
    specs = lambda at: [
        pl.BlockSpec((G * L, M_WIDTH), lambda s, c: (at(s, c), 0)),
        pl.BlockSpec((M_WIDTH, G * L), lambda s, c: (0, at(s, c))),
        pl.BlockSpec((M_WIDTH, G * L), lambda s, c: (1, at(s, c))),
        pl.BlockSpec((G * GS_ROWS, LANES), lambda s, c: (at(s, c), 0)),
        pl.BlockSpec((G * L, LANES), lambda s, c: (at(s, c), 0)),
    ]
    out_spec = lambda at: pl.BlockSpec((M_WIDTH, G * L), lambda s, c: (0, at(s, c)))
    return pl.pallas_call(
        _mlstm_kernel,
        grid=(n_seq, nb),
        in_specs=specs(fwd) + specs(bwd),
        out_specs=[out_spec(fwd), out_spec(bwd)],
        out_shape=[jax.ShapeDtypeStruct((M_WIDTH, n_tok), F32)] * 2,
        scratch_shapes=[
            pltpu.VMEM((2 * M_HEADS, M_HEAD_DIM + M_AUG, M_HEAD_DIM), F32),
            pltpu.VMEM((2 * M_HEADS, LANES), F32),
        ],
        compiler_params=pltpu.CompilerParams(
            dimension_semantics=("arbitrary", "arbitrary"), vmem_limit_bytes=VMEM_LIMIT),
        name="mlstm",
    )(k_tok, t_feat, t_feat, gs, uc, k_tok, t_feat, t_feat, gs, uc)


def _mixer_out_kernel(x_ref, xl_ref, nw_ref, hf_ref, hb_ref, mo_ref, mhw_ref, q_ref,
                      kp_ref, kc_ref, kn_ref, vp_ref, vc_ref, vn_ref, sink_ref,
                      wg_ref, wpm_ref, wpa_ref, wo_ref, out_ref, oa_s, ym_s, sa_s, *, n_tiles, tiles_per_seq):
    T, L = TOKEN_TILE, A_BLOCK
    j = pl.program_id(0)
    t_in_seq = jnp.minimum(j, n_tiles - 1) % tiles_per_seq

    @pl.when(j == 0)
    def _():
        oa_s[...] = jnp.zeros_like(oa_s)
        ym_s[...] = jnp.zeros_like(ym_s)
        sa_s[...] = jnp.zeros_like(sa_s)

    k_ext = jnp.concatenate([kp_ref[...], kc_ref[...], kn_ref[...]], axis=0)
    v_ext = jnp.concatenate([vp_ref[...], vc_ref[...], vn_ref[...]], axis=0)
    lane = lax.broadcasted_iota(jnp.int32, (1, LANES), 1)
    low = lane < A_HEAD_DIM
    n_blk = T // L
    W3 = 3 * L
    zero = jnp.zeros((), BF16)

    def scores(g):
        gs = slice(g * LANES, (g + 1) * LANES)
        kd = k_ext[:, gs]
        k_lo, k_hi = jnp.where(low, kd, zero), jnp.where(low, zero, kd)
        rows = []
        for i in range(n_blk):
            ks = slice(i * L, (i + 3) * L)
            kbd = jnp.concatenate([k_lo[ks], k_hi[ks]], axis=0)
            rows.append(lax.dot_general(q_ref[i * L:(i + 1) * L, gs], kbd, (((1,), (1,)), ((), ())),
                                        preferred_element_type=F32))
        return jnp.concatenate(rows, axis=0)

    def softmax(g, i, s):
        rs = slice(i * L, (i + 1) * L)
        p_parts, invs = [], []
        for sub in range(2):
            sink = sinks2[2 * g + sub:2 * g + sub + 1, 0:1]
            c0 = s[rs, sub * W3:sub * W3 + L] + bias_prev[i]
            c1 = s[rs, sub * W3 + L:sub * W3 + 2 * L]
            c2 = s[rs, sub * W3 + 2 * L:(sub + 1) * W3] + bias_next[i]
            m = jnp.maximum(jnp.max(jnp.maximum(jnp.maximum(c0, c1), c2), axis=-1, keepdims=True), sink)
            p0, p1, p2 = jnp.exp2(c0 - m), jnp.exp2(c1 - m), jnp.exp2(c2 - m)
            denom = jnp.sum(p0 + p1 + p2, axis=-1, keepdims=True) + jnp.exp2(sink - m)
            invs.append(1.0 / denom)
            p_parts += [p0.astype(BF16), p1.astype(BF16), p2.astype(BF16)]
        return jnp.concatenate(p_parts, axis=1), jnp.where(low, invs[0], invs[1])

    def attend(g, s):
        gs = slice(g * LANES, (g + 1) * LANES)
        vd = v_ext[:, gs]
        v_lo, v_hi = jnp.where(low, vd, zero), jnp.where(low, zero, vd)
        outs = []
        for i in range(n_blk):
            p, inv = softmax(g, i, s)
            ks = slice(i * L, (i + 3) * L)
            vbd = jnp.concatenate([v_lo[ks], v_hi[ks]], axis=0)
            outs.append((jnp.dot(p, vbd, preferred_element_type=F32) * inv).astype(BF16))
        return jnp.concatenate(outs, axis=0)

    def tail(rs):
        y_a = jnp.dot(oa_s[rs, :], wpa_ref[...], preferred_element_type=F32)
        y = (ym_s[rs, :] + sa_s[rs, :] * y_a).astype(BF16)
        out_ref[rs, :] = xl_ref[rs, :] + jnp.dot(y, wo_ref[...], preferred_element_type=F32)

    s = scores(0)

    xb = _rmsnorm(x_ref[...], nw_ref[...]).astype(BF16)
    qi = lax.broadcasted_iota(jnp.int32, (L, L), 0)
    kj = lax.broadcasted_iota(jnp.int32, (L, L), 1)
    tri_prev = jnp.where(kj >= qi, 0.0, -jnp.inf)
    tri_next = jnp.where(kj <= qi, 0.0, -jnp.inf)
    edge_prev = jnp.where((kj >= qi) & (t_in_seq > 0), 0.0, -jnp.inf)
    edge_next = jnp.where((kj <= qi) & (t_in_seq < tiles_per_seq - 1), 0.0, -jnp.inf)
    bias_prev = [edge_prev] + [tri_prev] * (n_blk - 1)
    bias_next = [tri_next] * (n_blk - 1) + [edge_next]
    sinks2 = sink_ref[...] * LOG2E

    hsum = hf_ref[...] + hb_ref[...]
    parts = []
    for h in range(M_HEADS):
        hs = slice(h * M_HEAD_DIM, (h + 1) * M_HEAD_DIM)
        hh = hsum[hs]
        hh = hh - jnp.mean(hh, axis=0, keepdims=True)
        hh = hh * lax.rsqrt(jnp.mean(hh * hh, axis=0, keepdims=True) + EPS)
        w = jnp.concatenate([mhw_ref[hs, :]] * (T // LANES), axis=1)
        parts.append((_sigmoid(mo_ref[hs, :].astype(F32)) * (hh * w)).astype(BF16))
    hg_t = jnp.concatenate(parts, axis=0)
    y_m = lax.dot_general(hg_t, wpm_ref[...], (((0,), (0,)), ((), ())), preferred_element_type=F32)

    o_cols = []
    s_next = scores(1)
    gate_m = jnp.dot(xb, wg_ref[:, :D_MODEL], preferred_element_type=F32)
    o_cols.append(attend(0, s))
    s, s_next = s_next, scores(2)
    gate_a = jnp.dot(xb, wg_ref[:, D_MODEL:], preferred_element_type=F32)
    o_cols.append(attend(1, s))
    s, s_next = s_next, scores(3)
    tail(slice(0, T // 2))
    o_cols.append(attend(2, s))
    y_m = _sigmoid(gate_m) * y_m
    tail(slice(T // 2, T))
    o_cols.append(attend(3, s_next))

    oa_s[...] = jnp.concatenate(o_cols, axis=1)
    ym_s[...] = y_m
    sa_s[...] = _sigmoid(gate_a)


def _mixer_out(x2, seq_len, nw, hf, hb, t_feat, mhw, aqkv, sink, wg, wpm, wpa, wo):
    n_tok = x2.shape[0]
    T, L = TOKEN_TILE, A_BLOCK
    n_tiles = n_tok // T
    tiles_per_seq = seq_len // T
    r = T // L
    n_lblk = n_tok // L
    cur = lambda i: jnp.minimum(i, n_tiles - 1)
    lag = lambda i: jnp.maximum(i - 1, 0)
    full = lambda a: _resident(a.shape)
    tile = lambda width, c: pl.BlockSpec((T, width), lambda i: (cur(i), c))
    feat = lambda c: pl.BlockSpec((M_WIDTH, T), lambda i: (c, cur(i)))
    prev = lambda c: pl.BlockSpec((L, A_WIDTH), lambda i: (jnp.maximum(cur(i) * r - 1, 0), c))
    nxt = lambda c: pl.BlockSpec((L, A_WIDTH), lambda i: (jnp.minimum((cur(i) + 1) * r, n_lblk - 1), c))
    lagged = pl.BlockSpec((T, D_MODEL), lambda i: (lag(i), 0))
    return pl.pallas_call(
        functools.partial(_mixer_out_kernel, n_tiles=n_tiles, tiles_per_seq=tiles_per_seq),
        grid=(n_tiles + 1,),
        in_specs=[
            tile(D_MODEL, 0), lagged, full(nw),
            feat(0), feat(0), feat(2), full(mhw),
            tile(A_WIDTH, 0),
            prev(1), tile(A_WIDTH, 1), nxt(1),
            prev(2), tile(A_WIDTH, 2), nxt(2),
            full(sink), full(wg), full(wpm), full(wpa), full(wo),
        ],
        out_specs=lagged,
        out_shape=jax.ShapeDtypeStruct((n_tok, D_MODEL), F32),
        scratch_shapes=[pltpu.VMEM((T, A_WIDTH), BF16), pltpu.VMEM((T, D_MODEL), F32),
                        pltpu.VMEM((T, D_MODEL), F32)],
        compiler_params=pltpu.CompilerParams(
            dimension_semantics=("arbitrary",), vmem_limit_bytes=VMEM_LIMIT),
        name="mixer_out",
    )(x2, x2, nw, hf, hb, t_feat, mhw, aqkv, aqkv, aqkv, aqkv, aqkv, aqkv, aqkv,
      sink, wg, wpm, wpa, wo)


def _conv_ffn_kernel(hp_ref, hc_ref, hn_ref, nw_ref, wua_ref, wub_ref, cw_ref, cb_ref, wd_ref,
                     nfw_ref, out_ref, xn_ref, z_ref, *, tiles_per_seq):
    T, HALO, C = FFN_TILE, FFN_HALO, FFN_CHUNK
    t_in_seq = pl.program_id(0) % tiles_per_seq
    nw = nw_ref[...]
    x = hc_ref[...]
    keep_prev = jnp.where(t_in_seq == 0, 0.0, 1.0)
    keep_next = jnp.where(t_in_seq == tiles_per_seq - 1, 0.0, 1.0)
    xn_ref[0:HALO, :] = (_rmsnorm(hp_ref[...], nw) * keep_prev).astype(BF16)
    xn_ref[HALO:HALO + T, :] = _rmsnorm(x, nw).astype(BF16)
    xn_ref[HALO + T:, :] = (_rmsnorm(hn_ref[...], nw) * keep_next).astype(BF16)
    xe = xn_ref[...]
    rows = T + 2 * HALO

    def conv(u, cw, cb):
        up = pltpu.roll(u, 1, 0)[HALO:HALO + T]
        un = pltpu.roll(u, rows - 1, 0)[HALO:HALO + T]
        return up * cw[0:1] + u[HALO:HALO + T] * cw[1:2] + un * cw[2:3] + cb

    def up(c):
        cs = slice(c * C, (c + 1) * C)
        return (jnp.dot(xe, wua_ref[:, cs], preferred_element_type=F32),
                jnp.dot(xe, wub_ref[:, cs], preferred_element_type=F32))

    def act(c, ua, ub):
        cs = slice(c * C, (c + 1) * C)
        gs = slice(D_FF + c * C, D_FF + (c + 1) * C)
        a = conv(ua, cw_ref[:, cs], cb_ref[:, cs])
        b = conv(ub, cw_ref[:, gs], cb_ref[:, gs])
        return (a * _sigmoid(a) * b).astype(BF16)

    n_chunks = D_FF // C
    u = up(0)
    for c in range(n_chunks):
        u_next = up(c + 1) if c + 1 < n_chunks else None
        z_ref[:, c * C:(c + 1) * C] = act(c, *u)
        u = u_next
    ffn = jnp.dot(z_ref[...], wd_ref[...], preferred_element_type=F32)
    out_ref[...] = _rmsnorm(x + ffn, nfw_ref[...])


def _conv_ffn(h1, seq_len, nw, wua, wub, cw, cb, wd, nfw):
    n_tok = h1.shape[0]
    T, HALO = FFN_TILE, FFN_HALO
    tiles_per_seq = seq_len // T
    r = T // HALO
    n_hblk = n_tok // HALO
    full = lambda a: _resident(a.shape)
    return pl.pallas_call(
        functools.partial(_conv_ffn_kernel, tiles_per_seq=tiles_per_seq),
        grid=(n_tok // T,),
        in_specs=[
            pl.BlockSpec((HALO, D_MODEL), lambda i: (jnp.maximum(i * r - 1, 0), 0)),
            pl.BlockSpec((T, D_MODEL), lambda i: (i, 0)),
            pl.BlockSpec((HALO, D_MODEL), lambda i: (jnp.minimum((i + 1) * r, n_hblk - 1), 0)),
            full(nw), full(wua), full(wub), full(cw), full(cb), full(wd), full(nfw),
        ],
        out_specs=pl.BlockSpec((T, D_MODEL), lambda i: (i, 0)),
        out_shape=jax.ShapeDtypeStruct((n_tok, D_MODEL), F32),
        scratch_shapes=[pltpu.VMEM((T + 2 * HALO, D_MODEL), BF16), pltpu.VMEM((T, D_FF), BF16)],
        compiler_params=pltpu.CompilerParams(
            dimension_semantics=("arbitrary",), vmem_limit_bytes=VMEM_LIMIT),
        name="conv_ffn",
    )(h1, h1, h1, nw, wua, wub, cw, cb, wd, nfw)


def _rope_table(seq_len):
    half = ROT_DIM // 2
    inv = ROPE_THETA ** (-jnp.arange(half, dtype=F32) / half)
    ang = jnp.arange(seq_len, dtype=F32)[:, None] * inv[None, :]
    cs = jnp.concatenate([jnp.cos(ang), jnp.sin(ang)], axis=1)
    return jnp.pad(cs, ((0, 0), (0, LANES - ROT_DIM)))


def _encoder(x, p, rope):
    n_seq, seq_len, _ = x.shape
    assert seq_len % max(IN_TILE, TOKEN_TILE, FFN_TILE, MLSTM_GROUP * M_CHUNK) == 0, seq_len
    assert (n_seq * seq_len) % (SCAN_CHUNKS * M_CHUNK) == 0, (n_seq, seq_len)
    x2 = x.reshape(n_seq * seq_len, D_MODEL)
    k_tok, t_feat, gate_rows, aqkv = _in_proj(x2, seq_len, p["norm1_w"], p["w_mk"], p["w_mt"],
                                              p["w_a"], p["gate_bias"], rope)
    gs, uc = _gate_scan(gate_rows)
    hf, hb = _mlstm(k_tok, t_feat, gs, uc, n_seq, seq_len)
    h1 = _mixer_out(x2, seq_len, p["norm1_w"], hf, hb, t_feat, p["mh_norm_w"], aqkv, p["sink"],
                    p["w_branch_gate"], p["w_proj_m"], p["w_proj_a"], p["w_out"])
    y = _conv_ffn(h1, seq_len, p["norm2_w"], p["w_up_a"], p["w_up_b"], p["conv_w"], p["conv_b"],
                  p["w_down"], p["norm_f_w"])
    return y.reshape(n_seq, seq_len, D_MODEL)


def _prepare_params(norm1_w, w_in, i_bias, f_bias, mh_norm_w, attn_sink, w_proj_m, w_proj_a,
                    w_out, norm2_w, w_up, conv_w, conv_b, w_down, norm_f_w):
    w_in = w_in[0]
    W = M_WIDTH
    m_end = 4 * W
    g_end = m_end + 4 * M_HEADS
    a_end = g_end + A_WIDTH + 2 * A_KV_WIDTH
    w_mt = jnp.concatenate([w_in[:, 0:W], w_in[:, 2 * W:4 * W], w_in[:, m_end:g_end]],
                           axis=1).astype(BF16).T
    gate_bias = jnp.concatenate([i_bias[0].reshape(-1), f_bias[0].reshape(-1)])
    return {
        "norm1_w": norm1_w[0].reshape(1, D_MODEL),
        "w_mk": w_in[:, W:2 * W].astype(BF16),
        "w_mt": w_mt,
        "gate_bias": jnp.broadcast_to(gate_bias[:, None], (4 * M_HEADS, LANES)),
        "w_a": w_in[:, g_end:a_end].astype(BF16),
        "w_branch_gate": w_in[:, a_end:].astype(BF16),
        "mh_norm_w": jnp.broadcast_to(mh_norm_w[0][:, None], (M_WIDTH, LANES)),
        "sink": jnp.broadcast_to(attn_sink[0][:, None], (A_HEADS, LANES)),
        "w_proj_m": w_proj_m[0].astype(BF16),
        "w_proj_a": w_proj_a[0].astype(BF16),
        "w_out": w_out[0].astype(BF16),
        "norm2_w": norm2_w[0].reshape(1, D_MODEL),
        "w_up_a": w_up[0][:, :D_FF].astype(BF16),
        "w_up_b": w_up[0][:, D_FF:].astype(BF16),
        "conv_w": conv_w[0],
        "conv_b": conv_b[0].reshape(1, 2 * D_FF),
        "w_down": w_down[0].astype(BF16),
        "norm_f_w": norm_f_w.reshape(1, D_MODEL),
    }


def kernel(x_prompt, x_sample, norm1_w, w_in, i_bias, f_bias, mh_norm_w, attn_sink, w_proj_m, w_proj_a,
           w_out, norm2_w, w_up, conv_w, conv_b, w_down, norm_f_w):
    p = _prepare_params(norm1_w, w_in, i_bias, f_bias, mh_norm_w, attn_sink, w_proj_m, w_proj_a,
                        w_out, norm2_w, w_up, conv_w, conv_b, w_down, norm_f_w)
    rope = _rope_table(max(x_prompt.shape[1], x_sample.shape[1]))
    return (_encoder(x_prompt, p, rope), _encoder(x_sample, p, rope))
```

```python
import functools

import jax
import jax.numpy as jnp
from jax import lax
from jax.experimental import pallas as pl
from jax.experimental.pallas import tpu as pltpu

F32 = jnp.float32
BF16 = jnp.bfloat16

D_MODEL = 1024
M_HEADS = 4
M_HEAD_DIM = 128
M_WIDTH = M_HEADS * M_HEAD_DIM
M_CHUNK = 128
A_HEADS = 8
A_KV_HEADS = 4
A_HEAD_DIM = 64
A_WIDTH = A_HEADS * A_HEAD_DIM
A_KV_WIDTH = A_KV_HEADS * A_HEAD_DIM
WINDOW = 128
A_BLOCK = 128
ROT_DIM = A_HEAD_DIM // 4
ROPE_THETA = 500000.0
D_FF = 2816
EPS = 1e-6
LOG2E = 1.4426950408889634

LANES = 128
IN_TILE = 1024
TOKEN_TILE = 512
FFN_TILE = 1024
FFN_HALO = 16
FFN_CHUNK = 256
M_AUG = 16
GS_ROWS = 5 * 2 * M_HEADS
SCAN_CHUNKS = 32
MLSTM_GROUP = 8
VMEM_LIMIT = 56 * 1024 * 1024


def _resident(shape):
    return pl.BlockSpec(shape, lambda *_: (0,) * len(shape), pipeline_mode=pl.Buffered(1))


def _rmsnorm(x, w):
    return x * lax.rsqrt(jnp.mean(x * x, axis=-1, keepdims=True) + EPS) * w


def _sigmoid(x):
    return 1.0 / (1.0 + jnp.exp(-x))


def _in_proj_kernel(x_ref, nw_ref, wk_ref, wt_ref, wa_ref, gb_ref, cs_ref,
                    k_out, t_out, g_out, a_out, xb_out):
    T = x_ref.shape[0]
    xb = _rmsnorm(x_ref[...], nw_ref[...]).astype(BF16)
    xb_out[...] = xb
    a = jnp.dot(xb, wa_ref[...], preferred_element_type=F32)
    k_out[...] = jnp.dot(xb, wk_ref[...], preferred_element_type=F32).astype(BF16)
    tt = lax.dot_general(wt_ref[...], xb, (((1,), (1,)), ((), ())),
                         preferred_element_type=F32)
    t_out[0:M_WIDTH, :] = (tt[0:M_WIDTH] * M_HEAD_DIM ** -0.5).astype(BF16)
    t_out[M_WIDTH:, :] = tt[M_WIDTH:3 * M_WIDTH].astype(BF16)
    g_out[...] = tt[3 * M_WIDTH:] + jnp.concatenate([gb_ref[...]] * (T // LANES), axis=1)

    half = ROT_DIM // 2
    lane = lax.broadcasted_iota(jnp.int32, (1, LANES), 1)
    low = lane < A_HEAD_DIM
    hl = lane & (A_HEAD_DIM - 1)
    first, second = hl < half, hl < ROT_DIM
    cs = cs_ref[...]
    cs2 = jnp.where(low, cs, pltpu.roll(cs, A_HEAD_DIM, 1))
    cos = jnp.where(first, cs2, jnp.where(second, pltpu.roll(cs2, half, 1), 1.0))
    sin = jnp.where(first, -pltpu.roll(cs2, LANES - half, 1), jnp.where(second, cs2, 0.0))

    def rope(t):
        partner = jnp.where(first, pltpu.roll(t, LANES - half, 1), pltpu.roll(t, half, 1))
        return t * cos + partner * sin

    def dup_heads(t):
        sw = pltpu.roll(t, A_HEAD_DIM, 1)
        return jnp.where(low, t, sw), jnp.where(low, sw, t)

    q_scale = A_HEAD_DIM ** -0.5 * LOG2E
    for j in range(A_WIDTH // LANES):
        a_out[:, j * LANES:(j + 1) * LANES] = (rope(a[:, j * LANES:(j + 1) * LANES]) * q_scale).astype(BF16)
    k_off, v_off = A_WIDTH, A_WIDTH + A_KV_WIDTH
    for j in range(A_KV_WIDTH // LANES):
        k0, k1 = dup_heads(rope(a[:, k_off + j * LANES:k_off + (j + 1) * LANES]))
        v0, v1 = dup_heads(a[:, v_off + j * LANES:v_off + (j + 1) * LANES])
        ko = A_WIDTH + 2 * j * LANES
        vo = 2 * A_WIDTH + 2 * j * LANES
        a_out[:, ko:ko + LANES] = k0.astype(BF16)
        a_out[:, ko + LANES:ko + 2 * LANES] = k1.astype(BF16)
        a_out[:, vo:vo + LANES] = v0.astype(BF16)
        a_out[:, vo + LANES:vo + 2 * LANES] = v1.astype(BF16)


def _in_proj(x2, seq_len, nw, wk, wt, wa, gb, cs_t):
    n_tok = x2.shape[0]
    T = IN_TILE
    tiles_per_seq = seq_len // T
    full = lambda a: _resident(a.shape)
    rope_spec = pl.BlockSpec((T, LANES), lambda i: (i % tiles_per_seq, 0))
    n_gate = 4 * M_HEADS
    return pl.pallas_call(
        _in_proj_kernel,
        grid=(n_tok // T,),
        in_specs=[
            pl.BlockSpec((T, D_MODEL), lambda i: (i, 0)),
            full(nw), full(wk), full(wt), full(wa), full(gb),
            rope_spec,
        ],
        out_specs=[
            pl.BlockSpec((T, M_WIDTH), lambda i: (i, 0)),
            pl.BlockSpec((3 * M_WIDTH, T), lambda i: (0, i)),
            pl.BlockSpec((n_gate, T), lambda i: (0, i)),
            pl.BlockSpec((T, 3 * A_WIDTH), lambda i: (i, 0)),
            pl.BlockSpec((T, D_MODEL), lambda i: (i, 0)),
        ],
        out_shape=[
            jax.ShapeDtypeStruct((n_tok, M_WIDTH), BF16),
            jax.ShapeDtypeStruct((3 * M_WIDTH, n_tok), BF16),
            jax.ShapeDtypeStruct((n_gate, n_tok), F32),
            jax.ShapeDtypeStruct((n_tok, 3 * A_WIDTH), BF16),
            jax.ShapeDtypeStruct((n_tok, D_MODEL), BF16),
        ],
        compiler_params=pltpu.CompilerParams(
            dimension_semantics=("arbitrary",), vmem_limit_bytes=VMEM_LIMIT),
        name="in_proj",
    )(x2, nw, wk, wt, wa, gb, cs_t)


def _log_sigmoid(x):
    return jnp.minimum(x, 0.0) - jnp.log(1.0 + jnp.exp(-jnp.abs(x)))


def _scan_lanes(x, op, fill, reverse):
    n = x.shape[1]
    lane = lax.broadcasted_iota(jnp.int32, x.shape, 1)
    sh = 1
    while sh < n:
        if reverse:
            x = op(x, jnp.where(lane < n - sh, pltpu.roll(x, n - sh, 1), fill))
        else:
            x = op(x, jnp.where(lane >= sh, pltpu.roll(x, sh, 1), fill))
        sh *= 2
    return x


def _gate_scan_kernel(g_ref, gs_out, uc_out):
    L, R = M_CHUNK, 2 * M_HEADS
    n_chunk = g_ref.shape[1] // L
    ig = jnp.concatenate([g_ref[0:R, j * L:(j + 1) * L] for j in range(n_chunk)], axis=0)
    lf = _log_sigmoid(jnp.concatenate([g_ref[R:2 * R, j * L:(j + 1) * L] for j in range(n_chunk)], axis=0))
    rows = lax.broadcasted_iota(jnp.int32, ig.shape, 0)
    is_fwd = (rows & (R - 1)) < M_HEADS
    cum = _scan_lanes(lf, jnp.add, 0.0, False)
    tot = jnp.broadcast_to(cum[:, L - 1:L], cum.shape)
    b = jnp.where(is_fwd, cum, tot - cum + lf)
    u = ig - b
    pmax = _scan_lanes(u, jnp.maximum, -jnp.inf, False)
    cmax = jnp.where(is_fwd, pmax, _scan_lanes(u, jnp.maximum, -jnp.inf, True))
    umax = jnp.broadcast_to(pmax[:, L - 1:L], pmax.shape)
    wgt = jnp.exp(u - umax)
    mloc = tot + umax
    pad = jnp.zeros((L - R, L), F32)
    for j in range(n_chunk):
        rs = slice(j * R, (j + 1) * R)
        gs_out[j * GS_ROWS:(j + 1) * GS_ROWS, :] = jnp.concatenate(
            [b[rs], cmax[rs], wgt[rs], tot[rs], mloc[rs]], axis=0)
        uc_out[j * L:(j + 1) * L, :] = jnp.concatenate([u[rs], pad], axis=0).T


def _gate_scan(gt):
    n_tok = gt.shape[1]
    L = M_CHUNK
    blk = SCAN_CHUNKS * L
    return pl.pallas_call(
        _gate_scan_kernel,
        grid=(n_tok // blk,),
        in_specs=[pl.BlockSpec((gt.shape[0], blk), lambda i: (0, i))],
        out_specs=[pl.BlockSpec((SCAN_CHUNKS * GS_ROWS, LANES), lambda i: (i, 0)),
                   pl.BlockSpec((blk, LANES), lambda i: (i, 0))],
        out_shape=[jax.ShapeDtypeStruct((n_tok // L * GS_ROWS, LANES), F32),
                   jax.ShapeDtypeStruct((n_tok, LANES), F32)],
        compiler_params=pltpu.CompilerParams(
            dimension_semantics=("arbitrary",), vmem_limit_bytes=VMEM_LIMIT),
        name="gate_scan",
    )(gt)


def _mlstm_kernel(kf_ref, qf_ref, vf_ref, gsf_ref, ucf_ref, kb_ref, qb_ref, vb_ref, gsb_ref, ucb_ref,
                  hf_out, hb_out, cn_ref, m_ref):
    L, DH, H, G = M_CHUNK, M_HEAD_DIM, M_HEADS, MLSTM_GROUP
    R = 2 * H

    @pl.when(pl.program_id(1) == 0)
    def _():
        cn_ref[...] = jnp.zeros_like(cn_ref)
        m_ref[...] = jnp.zeros_like(m_ref)

    is_fwd = lax.broadcasted_iota(jnp.int32, (R, L), 0) < H
    si = lax.broadcasted_iota(jnp.int32, (L, L), 0)
    ti = lax.broadcasted_iota(jnp.int32, (L, L), 1)
    tri = (jnp.where(si <= ti, 0.0, -jnp.inf), jnp.where(si >= ti, 0.0, -jnp.inf))
    ones_rows = jnp.where(lax.broadcasted_iota(jnp.int32, (M_AUG, L), 0) == 0, 1.0, 0.0).astype(BF16)
    row = lambda a, r: a[r:r + 1]

    for j in range(G):
        jf, jb = j, G - 1 - j
        gsf = gsf_ref[jf * GS_ROWS:(jf + 1) * GS_ROWS, :]
        gsb = gsb_ref[jb * GS_ROWS:(jb + 1) * GS_ROWS, :]
        b, cmax, wgt, tot, mloc = (
            jnp.where(is_fwd, gsf[n * R:(n + 1) * R], gsb[n * R:(n + 1) * R]) for n in range(5))
        m_prev = m_ref[...]
        g = jnp.maximum(m_prev, cmax)
        rho = jnp.exp(cmax - g)
        omega = jnp.exp(m_prev - g)
        thr = jnp.exp(-(b + g))
        m_new = jnp.maximum(tot + m_prev, mloc)
        s_prev = jnp.exp(tot + m_prev - m_new)
        s_loc = jnp.exp(mloc - m_new)
        m_ref[...] = m_new

        def operands(r):
            d, h = divmod(r, H)
            k_ref, q_ref, v_ref, uc_ref, jj = ((kf_ref, qf_ref, vf_ref, ucf_ref, jf) if d == 0
                                               else (kb_ref, qb_ref, vb_ref, ucb_ref, jb))
            hs, ts = slice(h * DH, (h + 1) * DH), slice(jj * L, (jj + 1) * L)
            v_aug = jnp.concatenate([v_ref[hs, ts], ones_rows], axis=0)
            return k_ref[ts, hs], q_ref[hs, ts], v_aug, uc_ref[ts, r:r + 1]

        ops = [operands(r) for r in range(R)]
        vw = [(v_aug.astype(F32) * row(wgt, r)).astype(BF16) for r, (_, _, v_aug, _) in enumerate(ops)]
        zblk = jnp.zeros((L, DH), BF16)
        bdiag = lambda a, b: jnp.concatenate([jnp.concatenate([a, zblk], axis=1),
                                              jnp.concatenate([zblk, b], axis=1)], axis=0)
        s_t, cn_loc = [None] * R, [None] * R
        for r in range(0, R, 2):
            (k0, q0, _, _), (k1, q1, _, _) = ops[r], ops[r + 1]
            s2 = jnp.dot(jnp.concatenate([k0, k1], axis=1), bdiag(q0, q1), preferred_element_type=F32)
            s_t[r], s_t[r + 1] = s2[:, :L], s2[:, L:]
        for r in range(0, R, 2):
            c2 = jnp.dot(jnp.concatenate([vw[r], vw[r + 1]], axis=1), bdiag(ops[r][0], ops[r + 1][0]),
                         preferred_element_type=F32)
            cn_loc[r], cn_loc[r + 1] = c2[:, :DH], c2[:, DH:]
        rhs = []
        for r, (k, q_t, v_aug, u_col) in enumerate(ops):
            decay = jnp.exp(u_col + (tri[r // H] - row(cmax, r)))
            p_t = (s_t[r] * decay * row(rho, r)).astype(BF16)
            q_w = (q_t.astype(F32) * row(omega, r)).astype(BF16)
            rhs.append(jnp.concatenate([p_t, q_w], axis=0))
        for r, (k, q_t, v_aug, u_col) in enumerate(ops):
            d, h = divmod(r, H)
            cn = cn_ref[r]
            lhs = jnp.concatenate([v_aug, cn.astype(BF16)], axis=1)
            out = jnp.dot(lhs, rhs[r], preferred_element_type=F32)
            rec = 1.0 / jnp.maximum(jnp.abs(out[DH:DH + 1]), row(thr, r))
            h_out, jj = (hf_out, jf) if d == 0 else (hb_out, jb)
            h_out[h * DH:(h + 1) * DH, jj * L:(jj + 1) * L] = out[:DH] * rec
            cn_ref[r] = row(s_prev, r) * cn + row(s_loc, r) * cn_loc[r]


def _mlstm(k_tok, t_feat, gs, uc, n_seq, seq_len):
    n_tok = k_tok.shape[0]
    L, G = M_CHUNK, MLSTM_GROUP
    nb = seq_len // (L * G)
    fwd = lambda s, c: s * nb + c
    bwd = lambda s, c: s * nb + nb - 1 - c
    specs = lambda at: [
        pl.BlockSpec((G * L, M_WIDTH), lambda s, c: (at(s, c), 0)),
        pl.BlockSpec((M_WIDTH, G * L), lambda s, c: (0, at(s, c))),
        pl.BlockSpec((M_WIDTH, G * L), lambda s, c: (1, at(s, c))),
        pl.BlockSpec((G * GS_ROWS, LANES), lambda s, c: (at(s, c), 0)),
        pl.BlockSpec((G * L, LANES), lambda s, c: (at(s, c), 0)),
    ]
    out_spec = lambda at: pl.BlockSpec((M_WIDTH, G * L), lambda s, c: (0, at(s, c)))
    return pl.pallas_call(
        _mlstm_kernel,
        grid=(n_seq, nb),
        in_specs=specs(fwd) + specs(bwd),
        out_specs=[out_spec(fwd), out_spec(bwd)],
        out_shape=[jax.ShapeDtypeStruct((M_WIDTH, n_tok), F32)] * 2,
        scratch_shapes=[
            pltpu.VMEM((2 * M_HEADS, M_HEAD_DIM + M_AUG, M_HEAD_DIM), F32),
            pltpu.VMEM((2 * M_HEADS, LANES), F32),
        ],
        compiler_params=pltpu.CompilerParams(
            dimension_semantics=("arbitrary", "arbitrary"), vmem_limit_bytes=VMEM_LIMIT),
        name="mlstm",
    )(k_tok, t_feat, t_feat, gs, uc, k_tok, t_feat, t_feat, gs, uc)


def _mixer_out_kernel(xb_ref, xl_ref, hf_ref, hb_ref, mo_ref, mhw_ref, q_ref,
                      kp_ref, kc_ref, kn_ref, vp_ref, vc_ref, vn_ref, sink_ref,
                      wg_ref, wpm_ref, wpa_ref, wo_ref, out_ref, oa_s, ym_s, sa_s, *, n_tiles, tiles_per_seq):
    T, L = TOKEN_TILE, A_BLOCK
    j = pl.program_id(0)
    t_in_seq = jnp.minimum(j, n_tiles - 1) % tiles_per_seq

    @pl.when(j == 0)
    def _():
        oa_s[...] = jnp.zeros_like(oa_s)
        ym_s[...] = jnp.zeros_like(ym_s)
        sa_s[...] = jnp.zeros_like(sa_s)

    k_ext = jnp.concatenate([kp_ref[...], kc_ref[...], kn_ref[...]], axis=0)
    v_ext = jnp.concatenate([vp_ref[...], vc_ref[...], vn_ref[...]], axis=0)
    lane = lax.broadcasted_iota(jnp.int32, (1, LANES), 1)
    low = lane < A_HEAD_DIM
    n_blk = T // L
    W3 = 3 * L
    zero = jnp.zeros((), BF16)

    def scores(g):
        gs = slice(g * LANES, (g + 1) * LANES)
        kd = k_ext[:, gs]
        k_lo, k_hi = jnp.where(low, kd, zero), jnp.where(low, zero, kd)
        rows = []
        for i in range(n_blk):
            ks = slice(i * L, (i + 3) * L)
            kbd = jnp.concatenate([k_lo[ks], k_hi[ks]], axis=0)
            rows.append(lax.dot_general(q_ref[i * L:(i + 1) * L, gs], kbd, (((1,), (1,)), ((), ())),
                                        preferred_element_type=F32))
        return jnp.concatenate(rows, axis=0)

    def softmax(g, i, s):
        rs = slice(i * L, (i + 1) * L)
        p_parts, invs = [], []
        for sub in range(2):
            sink = sinks2[2 * g + sub:2 * g + sub + 1, 0:1]
            c0 = s[rs, sub * W3:sub * W3 + L] + bias_prev[i]
            c1 = s[rs, sub * W3 + L:sub * W3 + 2 * L]
            c2 = s[rs, sub * W3 + 2 * L:(sub + 1) * W3] + bias_next[i]
            m = jnp.maximum(jnp.max(jnp.maximum(jnp.maximum(c0, c1), c2), axis=-1, keepdims=True), sink)
            p0, p1, p2 = jnp.exp2(c0 - m), jnp.exp2(c1 - m), jnp.exp2(c2 - m)
            denom = jnp.sum(p0 + p1 + p2, axis=-1, keepdims=True) + jnp.exp2(sink - m)
            invs.append(1.0 / denom)
            p_parts += [p0.astype(BF16), p1.astype(BF16), p2.astype(BF16)]
        return jnp.concatenate(p_parts, axis=1), jnp.where(low, invs[0], invs[1])

    def attend(g, s):
        gs = slice(g * LANES, (g + 1) * LANES)
        vd = v_ext[:, gs]
        v_lo, v_hi = jnp.where(low, vd, zero), jnp.where(low, zero, vd)
        outs = []
        for i in range(n_blk):
            p, inv = softmax(g, i, s)
            ks = slice(i * L, (i + 3) * L)
            vbd = jnp.concatenate([v_lo[ks], v_hi[ks]], axis=0)
            outs.append((jnp.dot(p, vbd, preferred_element_type=F32) * inv).astype(BF16))
        return jnp.concatenate(outs, axis=0)

    def tail(rs):
        y_a = jnp.dot(oa_s[rs, :], wpa_ref[...], preferred_element_type=F32)
        y = (ym_s[rs, :] + sa_s[rs, :] * y_a).astype(BF16)
        out_ref[rs, :] = xl_ref[rs, :] + jnp.dot(y, wo_ref[...], preferred_element_type=F32)

    s = scores(0)

    xb = xb_ref[...]
    qi = lax.broadcasted_iota(jnp.int32, (L, L), 0)
    kj = lax.broadcasted_iota(jnp.int32, (L, L), 1)
    tri_prev = jnp.where(kj >= qi, 0.0, -jnp.inf)
    tri_next = jnp.where(kj <= qi, 0.0, -jnp.inf)
    edge_prev = jnp.where((kj >= qi) & (t_in_seq > 0), 0.0, -jnp.inf)
    edge_next = jnp.where((kj <= qi) & (t_in_seq < tiles_per_seq - 1), 0.0, -jnp.inf)
    bias_prev = [edge_prev] + [tri_prev] * (n_blk - 1)
    bias_next = [tri_next] * (n_blk - 1) + [edge_next]
    sinks2 = sink_ref[...] * LOG2E

    hsum = hf_ref[...] + hb_ref[...]
    parts = []
    for h in range(M_HEADS):
        hs = slice(h * M_HEAD_DIM, (h + 1) * M_HEAD_DIM)
        hh = hsum[hs]
        hh = hh - jnp.mean(hh, axis=0, keepdims=True)
        hh = hh * lax.rsqrt(jnp.mean(hh * hh, axis=0, keepdims=True) + EPS)
        w = jnp.concatenate([mhw_ref[hs, :]] * (T // LANES), axis=1)
        parts.append((_sigmoid(mo_ref[hs, :].astype(F32)) * (hh * w)).astype(BF16))
    hg_t = jnp.concatenate(parts, axis=0)
    y_m = lax.dot_general(hg_t, wpm_ref[...], (((0,), (0,)), ((), ())), preferred_element_type=F32)

    o_cols = []
    s_next = scores(1)
    gate_m = jnp.dot(xb, wg_ref[:, :D_MODEL], preferred_element_type=F32)
    o_cols.append(attend(0, s))
    s, s_next = s_next, scores(2)
    gate_a = jnp.dot(xb, wg_ref[:, D_MODEL:], preferred_element_type=F32)
    o_cols.append(attend(1, s))
    s, s_next = s_next, scores(3)
    tail(slice(0, T // 2))
    o_cols.append(attend(2, s))
    y_m = _sigmoid(gate_m) * y_m
    tail(slice(T // 2, T))
    o_cols.append(attend(3, s_next))

    oa_s[...] = jnp.concatenate(o_cols, axis=1)
    ym_s[...] = y_m
    sa_s[...] = _sigmoid(gate_a)


def _mixer_out(x2, xb, seq_len, hf, hb, t_feat, mhw, aqkv, sink, wg, wpm, wpa, wo):
    n_tok = x2.shape[0]
    T, L = TOKEN_TILE, A_BLOCK
    n_tiles = n_tok // T
    tiles_per_seq = seq_len // T
    r = T // L
    n_lblk = n_tok // L
    cur = lambda i: jnp.minimum(i, n_tiles - 1)
    lag = lambda i: jnp.maximum(i - 1, 0)
    full = lambda a: _resident(a.shape)
    tile = lambda width, c: pl.BlockSpec((T, width), lambda i: (cur(i), c))
    feat = lambda c: pl.BlockSpec((M_WIDTH, T), lambda i: (c, cur(i)))
    prev = lambda c: pl.BlockSpec((L, A_WIDTH), lambda i: (jnp.maximum(cur(i) * r - 1, 0), c))
    nxt = lambda c: pl.BlockSpec((L, A_WIDTH), lambda i: (jnp.minimum((cur(i) + 1) * r, n_lblk - 1), c))
    lagged = pl.BlockSpec((T, D_MODEL), lambda i: (lag(i), 0))
    return pl.pallas_call(
        functools.partial(_mixer_out_kernel, n_tiles=n_tiles, tiles_per_seq=tiles_per_seq),
        grid=(n_tiles + 1,),
        in_specs=[
            tile(D_MODEL, 0), lagged,
            feat(0), feat(0), feat(2), full(mhw),
            tile(A_WIDTH, 0),
            prev(1), tile(A_WIDTH, 1), nxt(1),
            prev(2), tile(A_WIDTH, 2), nxt(2),
            full(sink), full(wg), full(wpm), full(wpa), full(wo),
        ],
        out_specs=lagged,
        out_shape=jax.ShapeDtypeStruct((n_tok, D_MODEL), F32),
        scratch_shapes=[pltpu.VMEM((T, A_WIDTH), BF16), pltpu.VMEM((T, D_MODEL), F32),
                        pltpu.VMEM((T, D_MODEL), F32)],
        compiler_params=pltpu.CompilerParams(
            dimension_semantics=("arbitrary",), vmem_limit_bytes=VMEM_LIMIT),
        name="mixer_out",
    )(xb, x2, hf, hb, t_feat, mhw, aqkv, aqkv, aqkv, aqkv, aqkv, aqkv, aqkv,
      sink, wg, wpm, wpa, wo)


def _conv_ffn_kernel(hp_ref, hc_ref, hn_ref, nw_ref, wua_ref, wub_ref, cw_ref, cb_ref, wd_ref,
                     nfw_ref, out_ref, xn_ref, z_ref, *, tiles_per_seq):
    T, HALO, C = FFN_TILE, FFN_HALO, FFN_CHUNK
    t_in_seq = pl.program_id(0) % tiles_per_seq
    nw = nw_ref[...]
    x = hc_ref[...]
    keep_prev = jnp.where(t_in_seq == 0, 0.0, 1.0)
    keep_next = jnp.where(t_in_seq == tiles_per_seq - 1, 0.0, 1.0)
    xn_ref[0:HALO, :] = (_rmsnorm(hp_ref[...], nw) * keep_prev).astype(BF16)
    xn_ref[HALO:HALO + T, :] = _rmsnorm(x, nw).astype(BF16)
    xn_ref[HALO + T:, :] = (_rmsnorm(hn_ref[...], nw) * keep_next).astype(BF16)
    rows = T + 2 * HALO

    def conv(u, cw, cb):
        up = pltpu.roll(u, 1, 0)[HALO:HALO + T]
        un = pltpu.roll(u, rows - 1, 0)[HALO:HALO + T]
        return up * cw[0:1] + u[HALO:HALO + T] * cw[1:2] + un * cw[2:3] + cb

    def up(c):
        cs = slice(c * C, (c + 1) * C)
        return (jnp.dot(xn_ref[...], wua_ref[:, cs], preferred_element_type=F32),
                jnp.dot(xn_ref[...], wub_ref[:, cs], preferred_element_type=F32))

    def act(c, ua, ub):
        cs = slice(c * C, (c + 1) * C)
        gs = slice(D_FF + c * C, D_FF + (c + 1) * C)
        a = conv(ua, cw_ref[:, cs], cb_ref[:, cs])
        b = conv(ub, cw_ref[:, gs], cb_ref[:, gs])
        return (a * _sigmoid(a) * b).astype(BF16)

    n_chunks = D_FF // C
    u = up(0)
    for c in range(n_chunks):
        u_next = up(c + 1) if c + 1 < n_chunks else None
        z_ref[:, c * C:(c + 1) * C] = act(c, *u)
        u = u_next
    halves = (slice(0, T // 2), slice(T // 2, T))
    ffn = [jnp.dot(z_ref[rs, :], wd_ref[...], preferred_element_type=F32) for rs in halves]
    for rs, f in zip(halves, ffn):
        out_ref[rs, :] = _rmsnorm(x[rs] + f, nfw_ref[...])


def _conv_ffn(h1, seq_len, nw, wua, wub, cw, cb, wd, nfw):
    n_tok = h1.shape[0]
    T, HALO = FFN_TILE, FFN_HALO
    tiles_per_seq = seq_len // T
    r = T // HALO
    n_hblk = n_tok // HALO
    full = lambda a: _resident(a.shape)
    return pl.pallas_call(
        functools.partial(_conv_ffn_kernel, tiles_per_seq=tiles_per_seq),
        grid=(n_tok // T,),
        in_specs=[
            pl.BlockSpec((HALO, D_MODEL), lambda i: (jnp.maximum(i * r - 1, 0), 0)),
            pl.BlockSpec((T, D_MODEL), lambda i: (i, 0)),
            pl.BlockSpec((HALO, D_MODEL), lambda i: (jnp.minimum((i + 1) * r, n_hblk - 1), 0)),
            full(nw), full(wua), full(wub), full(cw), full(cb), full(wd), full(nfw),
        ],
        out_specs=pl.BlockSpec((T, D_MODEL), lambda i: (i, 0)),
        out_shape=jax.ShapeDtypeStruct((n_tok, D_MODEL), F32),
        scratch_shapes=[pltpu.VMEM((T + 2 * HALO, D_MODEL), BF16), pltpu.VMEM((T, D_FF), BF16)],
        compiler_params=pltpu.CompilerParams(
            dimension_semantics=("arbitrary",), vmem_limit_bytes=VMEM_LIMIT),
        name="conv_ffn",
    )(h1, h1, h1, nw, wua, wub, cw, cb, wd, nfw)


def _rope_table(seq_len):
    half = ROT_DIM // 2
    inv = ROPE_THETA ** (-jnp.arange(half, dtype=F32) / half)
    ang = jnp.arange(seq_len, dtype=F32)[:, None] * inv[None, :]
    cs = jnp.concatenate([jnp.cos(ang), jnp.sin(ang)], axis=1)
    return jnp.pad(cs, ((0, 0), (0, LANES - ROT_DIM)))


def _encoder(x, p, rope):
    n_seq, seq_len, _ = x.shape
    assert seq_len % max(IN_TILE, TOKEN_TILE, FFN_TILE, MLSTM_GROUP * M_CHUNK) == 0, seq_len
    assert (n_seq * seq_len) % (SCAN_CHUNKS * M_CHUNK) == 0, (n_seq, seq_len)
    x2 = x.reshape(n_seq * seq_len, D_MODEL)
    k_tok, t_feat, gate_rows, aqkv, xb = _in_proj(x2, seq_len, p["norm1_w"], p["w_mk"], p["w_mt"],
                                                  p["w_a"], p["gate_bias"], rope)
    gs, uc = _gate_scan(gate_rows)
    hf, hb = _mlstm(k_tok, t_feat, gs, uc, n_seq, seq_len)
    h1 = _mixer_out(x2, xb, seq_len, hf, hb, t_feat, p["mh_norm_w"], aqkv, p["sink"],
                    p["w_branch_gate"], p["w_proj_m"], p["w_proj_a"], p["w_out"])
    y = _conv_ffn(h1, seq_len, p["norm2_w"], p["w_up_a"], p["w_up_b"], p["conv_w"], p["conv_b"],
                  p["w_down"], p["norm_f_w"])
    return y.reshape(n_seq, seq_len, D_MODEL)


def _prepare_params(norm1_w, w_in, i_bias, f_bias, mh_norm_w, attn_sink, w_proj_m, w_proj_a,
                    w_out, norm2_w, w_up, conv_w, conv_b, w_down, norm_f_w):
    w_in = w_in[0]
    W = M_WIDTH
    m_end = 4 * W
    g_end = m_end + 4 * M_HEADS
    a_end = g_end + A_WIDTH + 2 * A_KV_WIDTH
    w_mt = jnp.concatenate([w_in[:, 0:W], w_in[:, 2 * W:4 * W], w_in[:, m_end:g_end]],
                           axis=1).astype(BF16).T
    gate_bias = jnp.concatenate([i_bias[0].reshape(-1), f_bias[0].reshape(-1)])
    return {
        "norm1_w": norm1_w[0].reshape(1, D_MODEL),
        "w_mk": w_in[:, W:2 * W].astype(BF16),
        "w_mt": w_mt,
        "gate_bias": jnp.broadcast_to(gate_bias[:, None], (4 * M_HEADS, LANES)),
        "w_a": w_in[:, g_end:a_end].astype(BF16),
        "w_branch_gate": w_in[:, a_end:].astype(BF16),
        "mh_norm_w": jnp.broadcast_to(mh_norm_w[0][:, None], (M_WIDTH, LANES)),
        "sink": jnp.broadcast_to(attn_sink[0][:, None], (A_HEADS, LANES)),
        "w_proj_m": w_proj_m[0].astype(BF16),
        "w_proj_a": w_proj_a[0].astype(BF16),
        "w_out": w_out[0].astype(BF16),
        "norm2_w": norm2_w[0].reshape(1, D_MODEL),
        "w_up_a": w_up[0][:, :D_FF].astype(BF16),
        "w_up_b": w_up[0][:, D_FF:].astype(BF16),
        "conv_w": conv_w[0],
        "conv_b": conv_b[0].reshape(1, 2 * D_FF),
        "w_down": w_down[0].astype(BF16),
        "norm_f_w": norm_f_w.reshape(1, D_MODEL),
    }


def kernel(x_prompt, x_sample, norm1_w, w_in, i_bias, f_bias, mh_norm_w, attn_sink, w_proj_m, w_proj_a,
           w_out, norm2_w, w_up, conv_w, conv_b, w_down, norm_f_w):
    p = _prepare_params(norm1_w, w_in, i_bias, f_bias, mh_norm_w, attn_sink, w_proj_m, w_proj_a,
                        w_out, norm2_w, w_up, conv_w, conv_b, w_down, norm_f_w)
    rope = _rope_table(max(x_prompt.shape[1], x_sample.shape[1]))
    return (_encoder(x_prompt, p, rope), _encoder(x_sample, p, rope))
```

```python
import functools

import jax
import jax.numpy as jnp
from jax import lax
from jax.experimental import pallas as pl
from jax.experimental.pallas import tpu as pltpu

F32 = jnp.float32
BF16 = jnp.bfloat16

D_MODEL = 1024
M_HEADS = 4
M_HEAD_DIM = 128
M_WIDTH = M_HEADS * M_HEAD_DIM
M_CHUNK = 128
A_HEADS = 8
A_KV_HEADS = 4
A_HEAD_DIM = 64
A_WIDTH = A_HEADS * A_HEAD_DIM
A_KV_WIDTH = A_KV_HEADS * A_HEAD_DIM
WINDOW = 128
A_BLOCK = 128
ROT_DIM = A_HEAD_DIM // 4
ROPE_THETA = 500000.0
D_FF = 2816
EPS = 1e-6
LOG2E = 1.4426950408889634

LANES = 128
IN_TILE = 1024
TOKEN_TILE = 512
FFN_TILE = 1024
FFN_HALO = 16
FFN_CHUNK = 256
M_AUG = 16
GS_ROWS = 5 * 2 * M_HEADS
SCAN_CHUNKS = 32
MLSTM_GROUP = 8
VMEM_LIMIT = 56 * 1024 * 1024


def _resident(shape):
    return pl.BlockSpec(shape, lambda *_: (0,) * len(shape), pipeline_mode=pl.Buffered(1))


def _rmsnorm(x, w):
    return x * lax.rsqrt(jnp.mean(x * x, axis=-1, keepdims=True) + EPS) * w


def _sigmoid(x):
    return 1.0 / (1.0 + jnp.exp(-x))


def _in_proj_kernel(x_ref, nw_ref, wk_ref, wt_ref, wa_ref, gb_ref, cs_ref,
                    k_out, t_out, g_out, a_out, xb_out):
    T = x_ref.shape[0]
    xb = _rmsnorm(x_ref[...], nw_ref[...]).astype(BF16)
    xb_out[...] = xb
    a = jnp.dot(xb, wa_ref[...], preferred_element_type=F32)
    k_out[...] = jnp.dot(xb, wk_ref[...], preferred_element_type=F32).astype(BF16)
    tt = lax.dot_general(wt_ref[...], xb, (((1,), (1,)), ((), ())),
                         preferred_element_type=F32)
    t_out[0:M_WIDTH, :] = (tt[0:M_WIDTH] * M_HEAD_DIM ** -0.5).astype(BF16)
    t_out[M_WIDTH:, :] = tt[M_WIDTH:3 * M_WIDTH].astype(BF16)
    g_out[...] = tt[3 * M_WIDTH:] + jnp.concatenate([gb_ref[...]] * (T // LANES), axis=1)

    half = ROT_DIM // 2
    lane = lax.broadcasted_iota(jnp.int32, (1, LANES), 1)
    low = lane < A_HEAD_DIM
    hl = lane & (A_HEAD_DIM - 1)
    first, second = hl < half, hl < ROT_DIM
    cs = cs_ref[...]
    cs2 = jnp.where(low, cs, pltpu.roll(cs, A_HEAD_DIM, 1))
    cos = jnp.where(first, cs2, jnp.where(second, pltpu.roll(cs2, half, 1), 1.0))
    sin = jnp.where(first, -pltpu.roll(cs2, LANES - half, 1), jnp.where(second, cs2, 0.0))

    def rope(t):
        partner = jnp.where(first, pltpu.roll(t, LANES - half, 1), pltpu.roll(t, half, 1))
        return t * cos + partner * sin

    def dup_heads(t):
        sw = pltpu.roll(t, A_HEAD_DIM, 1)
        return jnp.where(low, t, sw), jnp.where(low, sw, t)

    q_scale = A_HEAD_DIM ** -0.5 * LOG2E
    for j in range(A_WIDTH // LANES):
        a_out[:, j * LANES:(j + 1) * LANES] = (rope(a[:, j * LANES:(j + 1) * LANES]) * q_scale).astype(BF16)
    k_off, v_off = A_WIDTH, A_WIDTH + A_KV_WIDTH
    for j in range(A_KV_WIDTH // LANES):
        k0, k1 = dup_heads(rope(a[:, k_off + j * LANES:k_off + (j + 1) * LANES]))
        v0, v1 = dup_heads(a[:, v_off + j * LANES:v_off + (j + 1) * LANES])
        ko = A_WIDTH + 2 * j * LANES
        vo = 2 * A_WIDTH + 2 * j * LANES
        a_out[:, ko:ko + LANES] = k0.astype(BF16)
        a_out[:, ko + LANES:ko + 2 * LANES] = k1.astype(BF16)
        a_out[:, vo:vo + LANES] = v0.astype(BF16)
        a_out[:, vo + LANES:vo + 2 * LANES] = v1.astype(BF16)


def _in_proj(x2, seq_len, nw, wk, wt, wa, gb, cs_t):
    n_tok = x2.shape[0]
    T = IN_TILE
    tiles_per_seq = seq_len // T
    full = lambda a: _resident(a.shape)
    rope_spec = pl.BlockSpec((T, LANES), lambda i: (i % tiles_per_seq, 0))
    n_gate = 4 * M_HEADS
    return pl.pallas_call(
        _in_proj_kernel,
        grid=(n_tok // T,),
        in_specs=[
            pl.BlockSpec((T, D_MODEL), lambda i: (i, 0)),
            full(nw), full(wk), full(wt), full(wa), full(gb),
            rope_spec,
        ],
        out_specs=[
            pl.BlockSpec((T, M_WIDTH), lambda i: (i, 0)),
            pl.BlockSpec((3 * M_WIDTH, T), lambda i: (0, i)),
            pl.BlockSpec((n_gate, T), lambda i: (0, i)),
            pl.BlockSpec((T, 3 * A_WIDTH), lambda i: (i, 0)),
            pl.BlockSpec((T, D_MODEL), lambda i: (i, 0)),
        ],
        out_shape=[
            jax.ShapeDtypeStruct((n_tok, M_WIDTH), BF16),
            jax.ShapeDtypeStruct((3 * M_WIDTH, n_tok), BF16),
            jax.ShapeDtypeStruct((n_gate, n_tok), F32),
            jax.ShapeDtypeStruct((n_tok, 3 * A_WIDTH), BF16),
            jax.ShapeDtypeStruct((n_tok, D_MODEL), BF16),
        ],
        compiler_params=pltpu.CompilerParams(
            dimension_semantics=("arbitrary",), vmem_limit_bytes=VMEM_LIMIT),
        name="in_proj",
    )(x2, nw, wk, wt, wa, gb, cs_t)


def _log_sigmoid(x):
    return jnp.minimum(x, 0.0) - jnp.log(1.0 + jnp.exp(-jnp.abs(x)))


def _scan_lanes(x, op, fill, reverse):
    n = x.shape[1]
    lane = lax.broadcasted_iota(jnp.int32, x.shape, 1)
    sh = 1
    while sh < n:
        if reverse:
            x = op(x, jnp.where(lane < n - sh, pltpu.roll(x, n - sh, 1), fill))
        else:
            x = op(x, jnp.where(lane >= sh, pltpu.roll(x, sh, 1), fill))
        sh *= 2
    return x


def _gate_scan_kernel(g_ref, gs_out, uc_out):
    L, R = M_CHUNK, 2 * M_HEADS
    n_chunk = g_ref.shape[1] // L
    ig = jnp.concatenate([g_ref[0:R, j * L:(j + 1) * L] for j in range(n_chunk)], axis=0)
    lf = _log_sigmoid(jnp.concatenate([g_ref[R:2 * R, j * L:(j + 1) * L] for j in range(n_chunk)], axis=0))
    rows = lax.broadcasted_iota(jnp.int32, ig.shape, 0)
    is_fwd = (rows & (R - 1)) < M_HEADS
    cum = _scan_lanes(lf, jnp.add, 0.0, False)
    tot = jnp.broadcast_to(cum[:, L - 1:L], cum.shape)
    b = jnp.where(is_fwd, cum, tot - cum + lf)
    u = ig - b
    pmax = _scan_lanes(u, jnp.maximum, -jnp.inf, False)
    cmax = jnp.where(is_fwd, pmax, _scan_lanes(u, jnp.maximum, -jnp.inf, True))
    umax = jnp.broadcast_to(pmax[:, L - 1:L], pmax.shape)
    wgt = jnp.exp(u - umax)
    mloc = tot + umax
    pad = jnp.zeros((L - R, L), F32)
    for j in range(n_chunk):
        rs = slice(j * R, (j + 1) * R)
        gs_out[j * GS_ROWS:(j + 1) * GS_ROWS, :] = jnp.concatenate(
            [b[rs], cmax[rs], wgt[rs], tot[rs], mloc[rs]], axis=0)
        uc_out[j * L:(j + 1) * L, :] = jnp.concatenate([u[rs], pad], axis=0).T


def _gate_scan(gt):
    n_tok = gt.shape[1]
    L = M_CHUNK
    blk = SCAN_CHUNKS * L
    return pl.pallas_call(
        _gate_scan_kernel,
        grid=(n_tok // blk,),
        in_specs=[pl.BlockSpec((gt.shape[0], blk), lambda i: (0, i))],
        out_specs=[pl.BlockSpec((SCAN_CHUNKS * GS_ROWS, LANES), lambda i: (i, 0)),
                   pl.BlockSpec((blk, LANES), lambda i: (i, 0))],
        out_shape=[jax.ShapeDtypeStruct((n_tok // L * GS_ROWS, LANES), F32),
                   jax.ShapeDtypeStruct((n_tok, LANES), F32)],
        compiler_params=pltpu.CompilerParams(
            dimension_semantics=("arbitrary",), vmem_limit_bytes=VMEM_LIMIT),
        name="gate_scan",
    )(gt)


def _mlstm_kernel(kf_ref, qf_ref, vf_ref, gsf_ref, ucf_ref, kb_ref, qb_ref, vb_ref, gsb_ref, ucb_ref,
                  hf_out, hb_out, cn_ref, m_ref):
    L, DH, H, G = M_CHUNK, M_HEAD_DIM, M_HEADS, MLSTM_GROUP
    R = 2 * H

    @pl.when(pl.program_id(1) == 0)
    def _():
        cn_ref[...] = jnp.zeros_like(cn_ref)
        m_ref[...] = jnp.zeros_like(m_ref)

    is_fwd = lax.broadcasted_iota(jnp.int32, (R, L), 0) < H
    si = lax.broadcasted_iota(jnp.int32, (L, L), 0)
    ti = lax.broadcasted_iota(jnp.int32, (L, L), 1)
    tri = (jnp.where(si <= ti, 0.0, -jnp.inf), jnp.where(si >= ti, 0.0, -jnp.inf))
    ones_rows = jnp.where(lax.broadcasted_iota(jnp.int32, (M_AUG, L), 0) == 0, 1.0, 0.0).astype(BF16)
    row = lambda a, r: a[r:r + 1]

    for j in range(G):
        jf, jb = j, G - 1 - j
        gsf = gsf_ref[jf * GS_ROWS:(jf + 1) * GS_ROWS, :]
        gsb = gsb_ref[jb * GS_ROWS:(jb + 1) * GS_ROWS, :]
        b, cmax, wgt, tot, mloc = (
            jnp.where(is_fwd, gsf[n * R:(n + 1) * R], gsb[n * R:(n + 1) * R]) for n in range(5))
        m_prev = m_ref[...]
        g = jnp.maximum(m_prev, cmax)
        rho = jnp.exp(cmax - g)
        omega = jnp.exp(m_prev - g)
        thr = jnp.exp(-(b + g))
        m_new = jnp.maximum(tot + m_prev, mloc)
        s_prev = jnp.exp(tot + m_prev - m_new)
        s_loc = jnp.exp(mloc - m_new)
        m_ref[...] = m_new

        def operands(r):
            d, h = divmod(r, H)
            k_ref, q_ref, v_ref, uc_ref, jj = ((kf_ref, qf_ref, vf_ref, ucf_ref, jf) if d == 0
                                               else (kb_ref, qb_ref, vb_ref, ucb_ref, jb))
            hs, ts = slice(h * DH, (h + 1) * DH), slice(jj * L, (jj + 1) * L)
            v_aug = jnp.concatenate([v_ref[hs, ts], ones_rows], axis=0)
            return k_ref[ts, hs], q_ref[hs, ts], v_aug, uc_ref[ts, r:r + 1]

        ops = [operands(r) for r in range(R)]
        vw = [(v_aug.astype(F32) * row(wgt, r)).astype(BF16) for r, (_, _, v_aug, _) in enumerate(ops)]
        zblk = jnp.zeros((L, DH), BF16)
        bdiag = lambda a, b: jnp.concatenate([jnp.concatenate([a, zblk], axis=1),
                                              jnp.concatenate([zblk, b], axis=1)], axis=0)
        s_t, cn_loc = [None] * R, [None] * R
        for r in range(0, R, 2):
            (k0, q0, _, _), (k1, q1, _, _) = ops[r], ops[r + 1]
            s2 = jnp.dot(jnp.concatenate([k0, k1], axis=1), bdiag(q0, q1), preferred_element_type=F32)
            s_t[r], s_t[r + 1] = s2[:, :L], s2[:, L:]
        for r in range(0, R, 2):
            c2 = jnp.dot(jnp.concatenate([vw[r], vw[r + 1]], axis=1), bdiag(ops[r][0], ops[r + 1][0]),
                         preferred_element_type=F32)
            cn_loc[r], cn_loc[r + 1] = c2[:, :DH], c2[:, DH:]
        rhs = []
        for r, (k, q_t, v_aug, u_col) in enumerate(ops):
            decay = jnp.exp(u_col + (tri[r // H] - row(cmax, r)))
            p_t = (s_t[r] * decay * row(rho, r)).astype(BF16)
            q_w = (q_t.astype(F32) * row(omega, r)).astype(BF16)
            rhs.append(jnp.concatenate([p_t, q_w], axis=0))
        for r, (k, q_t, v_aug, u_col) in enumerate(ops):
            d, h = divmod(r, H)
            cn = cn_ref[r]
            lhs = jnp.concatenate([v_aug, cn.astype(BF16)], axis=1)
            out = jnp.dot(lhs, rhs[r], preferred_element_type=F32)
            rec = 1.0 / jnp.maximum(jnp.abs(out[DH:DH + 1]), row(thr, r))
            h_out, jj = (hf_out, jf) if d == 0 else (hb_out, jb)
            h_out[h * DH:(h + 1) * DH, jj * L:(jj + 1) * L] = out[:DH] * rec
            cn_ref[r] = row(s_prev, r) * cn + row(s_loc, r) * cn_loc[r]


def _mlstm(k_tok, t_feat, gs, uc, n_seq, seq_len):
    n_tok = k_tok.shape[0]
    L, G = M_CHUNK, MLSTM_GROUP
    nb = seq_len // (L * G)
    fwd = lambda s, c: s * nb + c
    bwd = lambda s, c: s * nb + nb - 1 - c
    specs = lambda at: [
        pl.BlockSpec((G * L, M_WIDTH), lambda s, c: (at(s, c), 0)),
        pl.BlockSpec((M_WIDTH, G * L), lambda s, c: (0, at(s, c))),
        pl.BlockSpec((M_WIDTH, G * L), lambda s, c: (1, at(s, c))),
        pl.BlockSpec((G * GS_ROWS, LANES), lambda s, c: (at(s, c), 0)),
        pl.BlockSpec((G * L, LANES), lambda s, c: (at(s, c), 0)),
    ]
    out_spec = lambda at: pl.BlockSpec((M_WIDTH, G * L), lambda s, c: (0, at(s, c)))
    return pl.pallas_call(
        _mlstm_kernel,
        grid=(n_seq, nb),
        in_specs=specs(fwd) + specs(bwd),
        out_specs=[out_spec(fwd), out_spec(bwd)],
        out_shape=[jax.ShapeDtypeStruct((M_WIDTH, n_tok), F32)] * 2,
        scratch_shapes=[
            pltpu.VMEM((2 * M_HEADS, M_HEAD_DIM + M_AUG, M_HEAD_DIM), F32),
            pltpu.VMEM((2 * M_HEADS, LANES), F32),
        ],
        compiler_params=pltpu.CompilerParams(
            dimension_semantics=("arbitrary", "arbitrary"), vmem_limit_bytes=VMEM_LIMIT),
        name="mlstm",
    )(k_tok, t_feat, t_feat, gs, uc, k_tok, t_feat, t_feat, gs, uc)


def _mixer_out_kernel(xb_ref, xl_ref, hf_ref, hb_ref, mo_ref, mhw_ref, q_ref,
                      kp_ref, kc_ref, kn_ref, vp_ref, vc_ref, vn_ref, sink_ref,
                      wg_ref, wpm_ref, wpa_ref, wo_ref, out_ref, oa_s, ym_s, sa_s, *, n_tiles, tiles_per_seq):
    T, L = TOKEN_TILE, A_BLOCK
    j = pl.program_id(0)
    t_in_seq = jnp.minimum(j, n_tiles - 1) % tiles_per_seq

    @pl.when(j == 0)
    def _():
        oa_s[...] = jnp.zeros_like(oa_s)
        ym_s[...] = jnp.zeros_like(ym_s)
        sa_s[...] = jnp.zeros_like(sa_s)

    k_ext = jnp.concatenate([kp_ref[...], kc_ref[...], kn_ref[...]], axis=0)
    v_ext = jnp.concatenate([vp_ref[...], vc_ref[...], vn_ref[...]], axis=0)
    lane = lax.broadcasted_iota(jnp.int32, (1, LANES), 1)
    low = lane < A_HEAD_DIM
    n_blk = T // L
    W3 = 3 * L
    zero = jnp.zeros((), BF16)

    def scores(g):
        gs = slice(g * LANES, (g + 1) * LANES)
        kd = k_ext[:, gs]
        k_lo, k_hi = jnp.where(low, kd, zero), jnp.where(low, zero, kd)
        rows = []
        for i in range(n_blk):
            ks = slice(i * L, (i + 3) * L)
            kbd = jnp.concatenate([k_lo[ks], k_hi[ks]], axis=0)
            rows.append(lax.dot_general(q_ref[i * L:(i + 1) * L, gs], kbd, (((1,), (1,)), ((), ())),
                                        preferred_element_type=F32))
        return jnp.concatenate(rows, axis=0)

    def softmax(g, i, s):
        rs = slice(i * L, (i + 1) * L)
        p_parts, invs = [], []
        for sub in range(2):
            sink = sinks2[2 * g + sub:2 * g + sub + 1, 0:1]
            c0 = s[rs, sub * W3:sub * W3 + L] + bias_prev[i]
            c1 = s[rs, sub * W3 + L:sub * W3 + 2 * L]
            c2 = s[rs, sub * W3 + 2 * L:(sub + 1) * W3] + bias_next[i]
            m = jnp.maximum(jnp.max(jnp.maximum(jnp.maximum(c0, c1), c2), axis=-1, keepdims=True), sink)
            p0, p1, p2 = jnp.exp2(c0 - m), jnp.exp2(c1 - m), jnp.exp2(c2 - m)
            denom = jnp.sum(p0 + p1 + p2, axis=-1, keepdims=True) + jnp.exp2(sink - m)
            invs.append(1.0 / denom)
            p_parts += [p0.astype(BF16), p1.astype(BF16), p2.astype(BF16)]
        return jnp.concatenate(p_parts, axis=1), jnp.where(low, invs[0], invs[1])

    def attend(g, s):
        gs = slice(g * LANES, (g + 1) * LANES)
        vd = v_ext[:, gs]
        v_lo, v_hi = jnp.where(low, vd, zero), jnp.where(low, zero, vd)
        outs = []
        for i in range(n_blk):
            p, inv = softmax(g, i, s)
            ks = slice(i * L, (i + 3) * L)
            vbd = jnp.concatenate([v_lo[ks], v_hi[ks]], axis=0)
            outs.append((jnp.dot(p, vbd, preferred_element_type=F32) * inv).astype(BF16))
        return jnp.concatenate(outs, axis=0)

    def tail(rs):
        y_a = jnp.dot(oa_s[rs, :], wpa_ref[...], preferred_element_type=F32)
        y = (ym_s[rs, :] + sa_s[rs, :] * y_a).astype(BF16)
        out_ref[rs, :] = xl_ref[rs, :] + jnp.dot(y, wo_ref[...], preferred_element_type=F32)

    s = scores(0)

    xb = xb_ref[...]
    qi = lax.broadcasted_iota(jnp.int32, (L, L), 0)
    kj = lax.broadcasted_iota(jnp.int32, (L, L), 1)
    tri_prev = jnp.where(kj >= qi, 0.0, -jnp.inf)
    tri_next = jnp.where(kj <= qi, 0.0, -jnp.inf)
    edge_prev = jnp.where((kj >= qi) & (t_in_seq > 0), 0.0, -jnp.inf)
    edge_next = jnp.where((kj <= qi) & (t_in_seq < tiles_per_seq - 1), 0.0, -jnp.inf)
    bias_prev = [edge_prev] + [tri_prev] * (n_blk - 1)
    bias_next = [tri_next] * (n_blk - 1) + [edge_next]
    sinks2 = sink_ref[...] * LOG2E

    def mlstm_branch():
        hsum = hf_ref[...] + hb_ref[...]
        parts = []
        for h in range(M_HEADS):
            hs = slice(h * M_HEAD_DIM, (h + 1) * M_HEAD_DIM)
            hh = hsum[hs]
            hh = hh - jnp.mean(hh, axis=0, keepdims=True)
            hh = hh * lax.rsqrt(jnp.mean(hh * hh, axis=0, keepdims=True) + EPS)
            w = jnp.concatenate([mhw_ref[hs, :]] * (T // LANES), axis=1)
            parts.append((_sigmoid(mo_ref[hs, :].astype(F32)) * (hh * w)).astype(BF16))
        hg_t = jnp.concatenate(parts, axis=0)
        return lax.dot_general(hg_t, wpm_ref[...], (((0,), (0,)), ((), ())), preferred_element_type=F32)

    o_cols = []
    s_next = scores(1)
    gate_m = jnp.dot(xb, wg_ref[:, :D_MODEL], preferred_element_type=F32)
    o_cols.append(attend(0, s))
    y_m = mlstm_branch()
    s, s_next = s_next, scores(2)
    tail(slice(0, T // 2))
    o_cols.append(attend(1, s))
    s, s_next = s_next, scores(3)
    gate_a = jnp.dot(xb, wg_ref[:, D_MODEL:], preferred_element_type=F32)
    o_cols.append(attend(2, s))
    tail(slice(T // 2, T))
    o_cols.append(attend(3, s_next))
    y_m = _sigmoid(gate_m) * y_m

    oa_s[...] = jnp.concatenate(o_cols, axis=1)
    ym_s[...] = y_m
    sa_s[...] = _sigmoid(gate_a)


def _mixer_out(x2, xb, seq_len, hf, hb, t_feat, mhw, aqkv, sink, wg, wpm, wpa, wo):
    n_tok = x2.shape[0]
    T, L = TOKEN_TILE, A_BLOCK
    n_tiles = n_tok // T
    tiles_per_seq = seq_len // T
    r = T // L
    n_lblk = n_tok // L
    cur = lambda i: jnp.minimum(i, n_tiles - 1)
    lag = lambda i: jnp.maximum(i - 1, 0)
    full = lambda a: _resident(a.shape)
    tile = lambda width, c: pl.BlockSpec((T, width), lambda i: (cur(i), c))
    feat = lambda c: pl.BlockSpec((M_WIDTH, T), lambda i: (c, cur(i)))
    prev = lambda c: pl.BlockSpec((L, A_WIDTH), lambda i: (jnp.maximum(cur(i) * r - 1, 0), c))
    nxt = lambda c: pl.BlockSpec((L, A_WIDTH), lambda i: (jnp.minimum((cur(i) + 1) * r, n_lblk - 1), c))
    lagged = pl.BlockSpec((T, D_MODEL), lambda i: (lag(i), 0))
    return pl.pallas_call(
        functools.partial(_mixer_out_kernel, n_tiles=n_tiles, tiles_per_seq=tiles_per_seq),
        grid=(n_tiles + 1,),
        in_specs=[
            tile(D_MODEL, 0), lagged,
            feat(0), feat(0), feat(2), full(mhw),
            tile(A_WIDTH, 0),
            prev(1), tile(A_WIDTH, 1), nxt(1),
            prev(2), tile(A_WIDTH, 2), nxt(2),
            full(sink), full(wg), full(wpm), full(wpa), full(wo),
        ],
        out_specs=lagged,
        out_shape=jax.ShapeDtypeStruct((n_tok, D_MODEL), F32),
        scratch_shapes=[pltpu.VMEM((T, A_WIDTH), BF16), pltpu.VMEM((T, D_MODEL), F32),
                        pltpu.VMEM((T, D_MODEL), F32)],
        compiler_params=pltpu.CompilerParams(
            dimension_semantics=("arbitrary",), vmem_limit_bytes=VMEM_LIMIT),
        name="mixer_out",
    )(xb, x2, hf, hb, t_feat, mhw, aqkv, aqkv, aqkv, aqkv, aqkv, aqkv, aqkv,
      sink, wg, wpm, wpa, wo)


def _conv_ffn_kernel(hp_ref, hc_ref, hn_ref, nw_ref, wua_ref, wub_ref, cw_ref, cb_ref, wd_ref,
                     nfw_ref, out_ref, xn_ref, z_ref, *, tiles_per_seq):
    T, HALO, C = FFN_TILE, FFN_HALO, FFN_CHUNK
    t_in_seq = pl.program_id(0) % tiles_per_seq
    nw = nw_ref[...]
    x = hc_ref[...]
    keep_prev = jnp.where(t_in_seq == 0, 0.0, 1.0)
    keep_next = jnp.where(t_in_seq == tiles_per_seq - 1, 0.0, 1.0)
    xn_ref[0:HALO, :] = (_rmsnorm(hp_ref[...], nw) * keep_prev).astype(BF16)
    xn_ref[HALO:HALO + T, :] = _rmsnorm(x, nw).astype(BF16)
    xn_ref[HALO + T:, :] = (_rmsnorm(hn_ref[...], nw) * keep_next).astype(BF16)
    rows = T + 2 * HALO

    def conv(u, cw, cb):
        up = pltpu.roll(u, 1, 0)[HALO:HALO + T]
        un = pltpu.roll(u, rows - 1, 0)[HALO:HALO + T]
        return up * cw[0:1] + u[HALO:HALO + T] * cw[1:2] + un * cw[2:3] + cb

    def up(c):
        cs = slice(c * C, (c + 1) * C)
        return (jnp.dot(xn_ref[...], wua_ref[:, cs], preferred_element_type=F32),
                jnp.dot(xn_ref[...], wub_ref[:, cs], preferred_element_type=F32))

    def act(c, ua, ub):
        cs = slice(c * C, (c + 1) * C)
        gs = slice(D_FF + c * C, D_FF + (c + 1) * C)
        a = conv(ua, cw_ref[:, cs], cb_ref[:, cs])
        b = conv(ub, cw_ref[:, gs], cb_ref[:, gs])
        return (a * _sigmoid(a) * b).astype(BF16)

    n_chunks = D_FF // C
    u = up(0)
    for c in range(n_chunks):
        u_next = up(c + 1) if c + 1 < n_chunks else None
        z_ref[:, c * C:(c + 1) * C] = act(c, *u)
        u = u_next
    halves = (slice(0, T // 2), slice(T // 2, T))
    ffn = [jnp.dot(z_ref[rs, :], wd_ref[...], preferred_element_type=F32) for rs in halves]
    for rs, f in zip(halves, ffn):
        out_ref[rs, :] = _rmsnorm(x[rs] + f, nfw_ref[...])


def _conv_ffn(h1, seq_len, nw, wua, wub, cw, cb, wd, nfw):
    n_tok = h1.shape[0]
    T, HALO = FFN_TILE, FFN_HALO
    tiles_per_seq = seq_len // T
    r = T // HALO
    n_hblk = n_tok // HALO
    full = lambda a: _resident(a.shape)
    return pl.pallas_call(
        functools.partial(_conv_ffn_kernel, tiles_per_seq=tiles_per_seq),
        grid=(n_tok // T,),
        in_specs=[
            pl.BlockSpec((HALO, D_MODEL), lambda i: (jnp.maximum(i * r - 1, 0), 0)),
            pl.BlockSpec((T, D_MODEL), lambda i: (i, 0)),
            pl.BlockSpec((HALO, D_MODEL), lambda i: (jnp.minimum((i + 1) * r, n_hblk - 1), 0)),
            full(nw), full(wua), full(wub), full(cw), full(cb), full(wd), full(nfw),
        ],
        out_specs=pl.BlockSpec((T, D_MODEL), lambda i: (i, 0)),
        out_shape=jax.ShapeDtypeStruct((n_tok, D_MODEL), F32),
        scratch_shapes=[pltpu.VMEM((T + 2 * HALO, D_MODEL), BF16), pltpu.VMEM((T, D_FF), BF16)],
        compiler_params=pltpu.CompilerParams(
            dimension_semantics=("arbitrary",), vmem_limit_bytes=VMEM_LIMIT),
        name="conv_ffn",
    )(h1, h1, h1, nw, wua, wub, cw, cb, wd, nfw)


def _rope_table(seq_len):
    half = ROT_DIM // 2
    inv = ROPE_THETA ** (-jnp.arange(half, dtype=F32) / half)
    ang = jnp.arange(seq_len, dtype=F32)[:, None] * inv[None, :]
    cs = jnp.concatenate([jnp.cos(ang), jnp.sin(ang)], axis=1)
    return jnp.pad(cs, ((0, 0), (0, LANES - ROT_DIM)))


def _encoder(x, p, rope):
    n_seq, seq_len, _ = x.shape
    assert seq_len % max(IN_TILE, TOKEN_TILE, FFN_TILE, MLSTM_GROUP * M_CHUNK) == 0, seq_len
    assert (n_seq * seq_len) % (SCAN_CHUNKS * M_CHUNK) == 0, (n_seq, seq_len)
    x2 = x.reshape(n_seq * seq_len, D_MODEL)
    k_tok, t_feat, gate_rows, aqkv, xb = _in_proj(x2, seq_len, p["norm1_w"], p["w_mk"], p["w_mt"],
                                                  p["w_a"], p["gate_bias"], rope)
    gs, uc = _gate_scan(gate_rows)
    hf, hb = _mlstm(k_tok, t_feat, gs, uc, n_seq, seq_len)
    h1 = _mixer_out(x2, xb, seq_len, hf, hb, t_feat, p["mh_norm_w"], aqkv, p["sink"],
                    p["w_branch_gate"], p["w_proj_m"], p["w_proj_a"], p["w_out"])
    y = _conv_ffn(h1, seq_len, p["norm2_w"], p["w_up_a"], p["w_up_b"], p["conv_w"], p["conv_b"],
                  p["w_down"], p["norm_f_w"])
    return y.reshape(n_seq, seq_len, D_MODEL)


def _prepare_params(norm1_w, w_in, i_bias, f_bias, mh_norm_w, attn_sink, w_proj_m, w_proj_a,
                    w_out, norm2_w, w_up, conv_w, conv_b, w_down, norm_f_w):
    w_in = w_in[0]
    W = M_WIDTH
    m_end = 4 * W
    g_end = m_end + 4 * M_HEADS
    a_end = g_end + A_WIDTH + 2 * A_KV_WIDTH
    w_mt = jnp.concatenate([w_in[:, 0:W], w_in[:, 2 * W:4 * W], w_in[:, m_end:g_end]],
                           axis=1).astype(BF16).T
    gate_bias = jnp.concatenate([i_bias[0].reshape(-1), f_bias[0].reshape(-1)])
    return {
        "norm1_w": norm1_w[0].reshape(1, D_MODEL),
        "w_mk": w_in[:, W:2 * W].astype(BF16),
        "w_mt": w_mt,
        "gate_bias": jnp.broadcast_to(gate_bias[:, None], (4 * M_HEADS, LANES)),
        "w_a": w_in[:, g_end:a_end].astype(BF16),
        "w_branch_gate": w_in[:, a_end:].astype(BF16),
        "mh_norm_w": jnp.broadcast_to(mh_norm_w[0][:, None], (M_WIDTH, LANES)),
        "sink": jnp.broadcast_to(attn_sink[0][:, None], (A_HEADS, LANES)),
        "w_proj_m": w_proj_m[0].astype(BF16),
        "w_proj_a": w_proj_a[0].astype(BF16),
        "w_out": w_out[0].astype(BF16),
        "norm2_w": norm2_w[0].reshape(1, D_MODEL),
        "w_up_a": w_up[0][:, :D_FF].astype(BF16),
        "w_up_b": w_up[0][:, D_FF:].astype(BF16),
        "conv_w": conv_w[0],
        "conv_b": conv_b[0].reshape(1, 2 * D_FF),
        "w_down": w_down[0].astype(BF16),
        "norm_f_w": norm_f_w.reshape(1, D_MODEL),
    }


def kernel(x_prompt, x_sample, norm1_w, w_in, i_bias, f_bias, mh_norm_w, attn_sink, w_proj_m, w_proj_a,
           w_out, norm2_w, w_up, conv_w, conv_b, w_down, norm_f_w):
    p = _prepare_params(norm1_w, w_in, i_bias, f_bias, mh_norm_w, attn_sink, w_proj_m, w_proj_a,
                        w_out, norm2_w, w_up, conv_w, conv_b, w_down, norm_f_w)
    rope = _rope_table(max(x_prompt.shape[1], x_sample.shape[1]))
    return (_encoder(x_prompt, p, rope), _encoder(x_sample, p, rope))
```

```python
import functools

import jax
import jax.numpy as jnp
from jax import lax
from jax.experimental import pallas as pl
from jax.experimental.pallas import tpu as pltpu

F32 = jnp.float32
BF16 = jnp.bfloat16

D_MODEL = 1024
M_HEADS = 4
M_HEAD_DIM = 128
M_WIDTH = M_HEADS * M_HEAD_DIM
M_CHUNK = 128
A_HEADS = 8
A_KV_HEADS = 4
A_HEAD_DIM = 64
A_WIDTH = A_HEADS * A_HEAD_DIM
A_KV_WIDTH = A_KV_HEADS * A_HEAD_DIM
WINDOW = 128
A_BLOCK = 128
ROT_DIM = A_HEAD_DIM // 4
ROPE_THETA = 500000.0
D_FF = 2816
EPS = 1e-6
LOG2E = 1.4426950408889634

LANES = 128
IN_TILE = 1024
TOKEN_TILE = 512
FFN_TILE = 1024
FFN_HALO = 16
FFN_CHUNK = 256
M_AUG = 16
GS_ROWS = 5 * 2 * M_HEADS
SCAN_CHUNKS = 32
MLSTM_GROUP = 16
VMEM_LIMIT = 56 * 1024 * 1024


def _resident(shape):
    return pl.BlockSpec(shape, lambda *_: (0,) * len(shape), pipeline_mode=pl.Buffered(1))


def _rmsnorm(x, w):
    return x * lax.rsqrt(jnp.mean(x * x, axis=-1, keepdims=True) + EPS) * w


def _sigmoid(x):
    return 1.0 / (1.0 + jnp.exp(-x))


def _in_proj_kernel(x_ref, nw_ref, wk_ref, wt_ref, wa_ref, gb_ref, cs_ref,
                    k_out, t_out, g_out, a_out, xb_out):
    T = x_ref.shape[0]
    xb = _rmsnorm(x_ref[...], nw_ref[...]).astype(BF16)
    xb_out[...] = xb
    a = jnp.dot(xb, wa_ref[...], preferred_element_type=F32)
    k_out[...] = jnp.dot(xb, wk_ref[...], preferred_element_type=F32).astype(BF16)
    tt = lax.dot_general(wt_ref[...], xb, (((1,), (1,)), ((), ())),
                         preferred_element_type=F32)
    t_out[0:M_WIDTH, :] = (tt[0:M_WIDTH] * M_HEAD_DIM ** -0.5).astype(BF16)
    t_out[M_WIDTH:, :] = tt[M_WIDTH:3 * M_WIDTH].astype(BF16)
    g_out[...] = tt[3 * M_WIDTH:] + jnp.concatenate([gb_ref[...]] * (T // LANES), axis=1)

    half = ROT_DIM // 2
    lane = lax.broadcasted_iota(jnp.int32, (1, LANES), 1)
    low = lane < A_HEAD_DIM
    hl = lane & (A_HEAD_DIM - 1)
    first, second = hl < half, hl < ROT_DIM
    cs = cs_ref[...]
    cs2 = jnp.where(low, cs, pltpu.roll(cs, A_HEAD_DIM, 1))
    cos = jnp.where(first, cs2, jnp.where(second, pltpu.roll(cs2, half, 1), 1.0))
    sin = jnp.where(first, -pltpu.roll(cs2, LANES - half, 1), jnp.where(second, cs2, 0.0))

    def rope(t):
        partner = jnp.where(first, pltpu.roll(t, LANES - half, 1), pltpu.roll(t, half, 1))
        return t * cos + partner * sin

    def dup_heads(t):
        sw = pltpu.roll(t, A_HEAD_DIM, 1)
        return jnp.where(low, t, sw), jnp.where(low, sw, t)

    q_scale = A_HEAD_DIM ** -0.5 * LOG2E
    for j in range(A_WIDTH // LANES):
        a_out[:, j * LANES:(j + 1) * LANES] = (rope(a[:, j * LANES:(j + 1) * LANES]) * q_scale).astype(BF16)
    k_off, v_off = A_WIDTH, A_WIDTH + A_KV_WIDTH
    for j in range(A_KV_WIDTH // LANES):
        k0, k1 = dup_heads(rope(a[:, k_off + j * LANES:k_off + (j + 1) * LANES]))
        v0, v1 = dup_heads(a[:, v_off + j * LANES:v_off + (j + 1) * LANES])
        ko = A_WIDTH + 2 * j * LANES
        vo = 2 * A_WIDTH + 2 * j * LANES
        a_out[:, ko:ko + LANES] = k0.astype(BF16)
        a_out[:, ko + LANES:ko + 2 * LANES] = k1.astype(BF16)
        a_out[:, vo:vo + LANES] = v0.astype(BF16)
        a_out[:, vo + LANES:vo + 2 * LANES] = v1.astype(BF16)


def _in_proj(x2, seq_len, nw, wk, wt, wa, gb, cs_t):
    n_tok = x2.shape[0]
    T = IN_TILE
    tiles_per_seq = seq_len // T
    full = lambda a: _resident(a.shape)
    rope_spec = pl.BlockSpec((T, LANES), lambda i: (i % tiles_per_seq, 0))
    n_gate = 4 * M_HEADS
    return pl.pallas_call(
        _in_proj_kernel,
        grid=(n_tok // T,),
        in_specs=[
            pl.BlockSpec((T, D_MODEL), lambda i: (i, 0)),
            full(nw), full(wk), full(wt), full(wa), full(gb),
            rope_spec,
        ],
        out_specs=[
            pl.BlockSpec((T, M_WIDTH), lambda i: (i, 0)),
            pl.BlockSpec((3 * M_WIDTH, T), lambda i: (0, i)),
            pl.BlockSpec((n_gate, T), lambda i: (0, i)),
            pl.BlockSpec((T, 3 * A_WIDTH), lambda i: (i, 0)),
            pl.BlockSpec((T, D_MODEL), lambda i: (i, 0)),
        ],
        out_shape=[
            jax.ShapeDtypeStruct((n_tok, M_WIDTH), BF16),
            jax.ShapeDtypeStruct((3 * M_WIDTH, n_tok), BF16),
            jax.ShapeDtypeStruct((n_gate, n_tok), F32),
            jax.ShapeDtypeStruct((n_tok, 3 * A_WIDTH), BF16),
            jax.ShapeDtypeStruct((n_tok, D_MODEL), BF16),
        ],
        compiler_params=pltpu.CompilerParams(
            dimension_semantics=("arbitrary",), vmem_limit_bytes=VMEM_LIMIT),
        name="in_proj",
    )(x2, nw, wk, wt, wa, gb, cs_t)


def _log_sigmoid(x):
    return jnp.minimum(x, 0.0) - jnp.log(1.0 + jnp.exp(-jnp.abs(x)))


def _scan_lanes(x, op, fill, reverse):
    n = x.shape[1]
    lane = lax.broadcasted_iota(jnp.int32, x.shape, 1)
    sh = 1
    while sh < n:
        if reverse:
            x = op(x, jnp.where(lane < n - sh, pltpu.roll(x, n - sh, 1), fill))
        else:
            x = op(x, jnp.where(lane >= sh, pltpu.roll(x, sh, 1), fill))
        sh *= 2
    return x


def _gate_scan_kernel(g_ref, gs_out, uc_out):
    L, R = M_CHUNK, 2 * M_HEADS
    n_chunk = g_ref.shape[1] // L
    ig = jnp.concatenate([g_ref[0:R, j * L:(j + 1) * L] for j in range(n_chunk)], axis=0)
    lf = _log_sigmoid(jnp.concatenate([g_ref[R:2 * R, j * L:(j + 1) * L] for j in range(n_chunk)], axis=0))
    rows = lax.broadcasted_iota(jnp.int32, ig.shape, 0)
    is_fwd = (rows & (R - 1)) < M_HEADS
    cum = _scan_lanes(lf, jnp.add, 0.0, False)
    tot = jnp.broadcast_to(cum[:, L - 1:L], cum.shape)
    b = jnp.where(is_fwd, cum, tot - cum + lf)
    u = ig - b
    pmax = _scan_lanes(u, jnp.maximum, -jnp.inf, False)
    cmax = jnp.where(is_fwd, pmax, _scan_lanes(u, jnp.maximum, -jnp.inf, True))
    umax = jnp.broadcast_to(pmax[:, L - 1:L], pmax.shape)
    wgt = jnp.exp(u - umax)
    mloc = tot + umax
    pad = jnp.zeros((L - R, L), F32)
    for j in range(n_chunk):
        rs = slice(j * R, (j + 1) * R)
        gs_out[j * GS_ROWS:(j + 1) * GS_ROWS, :] = jnp.concatenate(
            [b[rs], cmax[rs], wgt[rs], tot[rs], mloc[rs]], axis=0)
        uc_out[j * L:(j + 1) * L, :] = jnp.concatenate([u[rs], pad], axis=0).T


def _gate_scan(gt):
    n_tok = gt.shape[1]
    L = M_CHUNK
    blk = SCAN_CHUNKS * L
    return pl.pallas_call(
        _gate_scan_kernel,
        grid=(n_tok // blk,),
        in_specs=[pl.BlockSpec((gt.shape[0], blk), lambda i: (0, i))],
        out_specs=[pl.BlockSpec((SCAN_CHUNKS * GS_ROWS, LANES), lambda i: (i, 0)),
                   pl.BlockSpec((blk, LANES), lambda i: (i, 0))],
        out_shape=[jax.ShapeDtypeStruct((n_tok // L * GS_ROWS, LANES), F32),
                   jax.ShapeDtypeStruct((n_tok, LANES), F32)],
        compiler_params=pltpu.CompilerParams(
            dimension_semantics=("arbitrary",), vmem_limit_bytes=VMEM_LIMIT),
        name="gate_scan",
    )(gt)


def _mlstm_kernel(kf_ref, qf_ref, vf_ref, gsf_ref, ucf_ref, kb_ref, qb_ref, vb_ref, gsb_ref, ucb_ref,
                  hf_out, hb_out, cn_ref, m_ref):
    L, DH, H, G = M_CHUNK, M_HEAD_DIM, M_HEADS, MLSTM_GROUP
    R = 2 * H

    @pl.when(pl.program_id(1) == 0)
    def _():
        cn_ref[...] = jnp.zeros_like(cn_ref)
        m_ref[...] = jnp.zeros_like(m_ref)

    is_fwd = lax.broadcasted_iota(jnp.int32, (R, L), 0) < H
    si = lax.broadcasted_iota(jnp.int32, (L, L), 0)
    ti = lax.broadcasted_iota(jnp.int32, (L, L), 1)
    tri = (jnp.where(si <= ti, 0.0, -jnp.inf), jnp.where(si >= ti, 0.0, -jnp.inf))
    ones_rows = jnp.where(lax.broadcasted_iota(jnp.int32, (M_AUG, L), 0) == 0, 1.0, 0.0).astype(BF16)
    row = lambda a, r: a[r:r + 1]

    for j in range(G):
        jf, jb = j, G - 1 - j
        gsf = gsf_ref[jf * GS_ROWS:(jf + 1) * GS_ROWS, :]
        gsb = gsb_ref[jb * GS_ROWS:(jb + 1) * GS_ROWS, :]
        b, cmax, wgt, tot, mloc = (
            jnp.where(is_fwd, gsf[n * R:(n + 1) * R], gsb[n * R:(n + 1) * R]) for n in range(5))
        m_prev = m_ref[...]
        g = jnp.maximum(m_prev, cmax)
        rho = jnp.exp(cmax - g)
        omega = jnp.exp(m_prev - g)
        thr = jnp.exp(-(b + g))
        m_new = jnp.maximum(tot + m_prev, mloc)
        s_prev = jnp.exp(tot + m_prev - m_new)
        s_loc = jnp.exp(mloc - m_new)
        m_ref[...] = m_new

        def operands(r):
            d, h = divmod(r, H)
            k_ref, q_ref, v_ref, uc_ref, jj = ((kf_ref, qf_ref, vf_ref, ucf_ref, jf) if d == 0
                                               else (kb_ref, qb_ref, vb_ref, ucb_ref, jb))
            hs, ts = slice(h * DH, (h + 1) * DH), slice(jj * L, (jj + 1) * L)
            v_aug = jnp.concatenate([v_ref[hs, ts], ones_rows], axis=0)
            return k_ref[ts, hs], q_ref[hs, ts], v_aug, uc_ref[ts, r:r + 1]

        ops = [operands(r) for r in range(R)]
        vw = [(v_aug.astype(F32) * row(wgt, r)).astype(BF16) for r, (_, _, v_aug, _) in enumerate(ops)]
        zblk = jnp.zeros((L, DH), BF16)
        bdiag = lambda a, b: jnp.concatenate([jnp.concatenate([a, zblk], axis=1),
                                              jnp.concatenate([zblk, b], axis=1)], axis=0)
        s_t, cn_loc = [None] * R, [None] * R
        for r in range(0, R, 2):
            (k0, q0, _, _), (k1, q1, _, _) = ops[r], ops[r + 1]
            s2 = jnp.dot(jnp.concatenate([k0, k1], axis=1), bdiag(q0, q1), preferred_element_type=F32)
            s_t[r], s_t[r + 1] = s2[:, :L], s2[:, L:]
        for r in range(0, R, 2):
            c2 = jnp.dot(jnp.concatenate([vw[r], vw[r + 1]], axis=1), bdiag(ops[r][0], ops[r + 1][0]),
                         preferred_element_type=F32)
            cn_loc[r], cn_loc[r + 1] = c2[:, :DH], c2[:, DH:]
        rhs = []
        for r, (k, q_t, v_aug, u_col) in enumerate(ops):
            decay = jnp.exp(u_col + (tri[r // H] - row(cmax, r)))
            p_t = (s_t[r] * decay * row(rho, r)).astype(BF16)
            q_w = (q_t.astype(F32) * row(omega, r)).astype(BF16)
            rhs.append(jnp.concatenate([p_t, q_w], axis=0))
        for r, (k, q_t, v_aug, u_col) in enumerate(ops):
            d, h = divmod(r, H)
            cn = cn_ref[r]
            lhs = jnp.concatenate([v_aug, cn.astype(BF16)], axis=1)
            out = jnp.dot(lhs, rhs[r], preferred_element_type=F32)
            rec = 1.0 / jnp.maximum(jnp.abs(out[DH:DH + 1]), row(thr, r))
            h_out, jj = (hf_out, jf) if d == 0 else (hb_out, jb)
            h_out[h * DH:(h + 1) * DH, jj * L:(jj + 1) * L] = out[:DH] * rec
            cn_ref[r] = row(s_prev, r) * cn + row(s_loc, r) * cn_loc[r]


def _mlstm(k_tok, t_feat, gs, uc, n_seq, seq_len):
    n_tok = k_tok.shape[0]
    L, G = M_CHUNK, MLSTM_GROUP
    nb = seq_len // (L * G)
    fwd = lambda s, c: s * nb + c
    bwd = lambda s, c: s * nb + nb - 1 - c
    specs = lambda at: [
        pl.BlockSpec((G * L, M_WIDTH), lambda s, c: (at(s, c), 0)),
        pl.BlockSpec((M_WIDTH, G * L), lambda s, c: (0, at(s, c))),
        pl.BlockSpec((M_WIDTH, G * L), lambda s, c: (1, at(s, c))),
        pl.BlockSpec((G * GS_ROWS, LANES), lambda s, c: (at(s, c), 0)),
        pl.BlockSpec((G * L, LANES), lambda s, c: (at(s, c), 0)),
    ]
    out_spec = lambda at: pl.BlockSpec((M_WIDTH, G * L), lambda s, c: (0, at(s, c)))
    return pl.pallas_call(
        _mlstm_kernel,
        grid=(n_seq, nb),
        in_specs=specs(fwd) + specs(bwd),
        out_specs=[out_spec(fwd), out_spec(bwd)],
        out_shape=[jax.ShapeDtypeStruct((M_WIDTH, n_tok), F32)] * 2,
        scratch_shapes=[
            pltpu.VMEM((2 * M_HEADS, M_HEAD_DIM + M_AUG, M_HEAD_DIM), F32),
            pltpu.VMEM((2 * M_HEADS, LANES), F32),
        ],
        compiler_params=pltpu.CompilerParams(
            dimension_semantics=("arbitrary", "arbitrary"), vmem_limit_bytes=VMEM_LIMIT),
        name="mlstm",
    )(k_tok, t_feat, t_feat, gs, uc, k_tok, t_feat, t_feat, gs, uc)


def _mixer_out_kernel(xb_ref, xl_ref, hf_ref, hb_ref, mo_ref, mhw_ref, q_ref,
                      kp_ref, kc_ref, kn_ref, vp_ref, vc_ref, vn_ref, sink_ref,
                      wg_ref, wpm_ref, wpa_ref, wo_ref, out_ref, oa_s, ym_s, sa_s, *, n_tiles, tiles_per_seq):
    T, L = TOKEN_TILE, A_BLOCK
    j = pl.program_id(0)
    t_in_seq = jnp.minimum(j, n_tiles - 1) % tiles_per_seq

    @pl.when(j == 0)
    def _():
        oa_s[...] = jnp.zeros_like(oa_s)
        ym_s[...] = jnp.zeros_like(ym_s)
        sa_s[...] = jnp.zeros_like(sa_s)

    k_ext = jnp.concatenate([kp_ref[...], kc_ref[...], kn_ref[...]], axis=0)
    v_ext = jnp.concatenate([vp_ref[...], vc_ref[...], vn_ref[...]], axis=0)
    lane = lax.broadcasted_iota(jnp.int32, (1, LANES), 1)
    low = lane < A_HEAD_DIM
    n_blk = T // L
    W3 = 3 * L
    zero = jnp.zeros((), BF16)

    def scores(g):
        gs = slice(g * LANES, (g + 1) * LANES)
        kd = k_ext[:, gs]
        k_lo, k_hi = jnp.where(low, kd, zero), jnp.where(low, zero, kd)
        rows = []
        for i in range(n_blk):
            ks = slice(i * L, (i + 3) * L)
            kbd = jnp.concatenate([k_lo[ks], k_hi[ks]], axis=0)
            rows.append(lax.dot_general(q_ref[i * L:(i + 1) * L, gs], kbd, (((1,), (1,)), ((), ())),
                                        preferred_element_type=F32))
        return jnp.concatenate(rows, axis=0)

    def softmax(g, i, s):
        rs = slice(i * L, (i + 1) * L)
        p_parts, invs = [], []
        for sub in range(2):
            sink = sinks2[2 * g + sub:2 * g + sub + 1, 0:1]
            c0 = s[rs, sub * W3:sub * W3 + L] + bias_prev[i]
            c1 = s[rs, sub * W3 + L:sub * W3 + 2 * L]
            c2 = s[rs, sub * W3 + 2 * L:(sub + 1) * W3] + bias_next[i]
            m = jnp.maximum(jnp.max(jnp.maximum(jnp.maximum(c0, c1), c2), axis=-1, keepdims=True), sink)
            p0, p1, p2 = jnp.exp2(c0 - m), jnp.exp2(c1 - m), jnp.exp2(c2 - m)
            denom = jnp.sum(p0 + p1 + p2, axis=-1, keepdims=True) + jnp.exp2(sink - m)
            invs.append(1.0 / denom)
            p_parts += [p0.astype(BF16), p1.astype(BF16), p2.astype(BF16)]
        return jnp.concatenate(p_parts, axis=1), jnp.where(low, invs[0], invs[1])

    def attend(g, s):
        gs = slice(g * LANES, (g + 1) * LANES)
        vd = v_ext[:, gs]
        v_lo, v_hi = jnp.where(low, vd, zero), jnp.where(low, zero, vd)
        outs = []
        for i in range(n_blk):
            p, inv = softmax(g, i, s)
            ks = slice(i * L, (i + 3) * L)
            vbd = jnp.concatenate([v_lo[ks], v_hi[ks]], axis=0)
            outs.append((jnp.dot(p, vbd, preferred_element_type=F32) * inv).astype(BF16))
        return jnp.concatenate(outs, axis=0)

    def tail(rs):
        y_a = jnp.dot(oa_s[rs, :], wpa_ref[...], preferred_element_type=F32)
        y = (ym_s[rs, :] + sa_s[rs, :] * y_a).astype(BF16)
        out_ref[rs, :] = xl_ref[rs, :] + jnp.dot(y, wo_ref[...], preferred_element_type=F32)

    s = scores(0)

    xb = xb_ref[...]
    qi = lax.broadcasted_iota(jnp.int32, (L, L), 0)
    kj = lax.broadcasted_iota(jnp.int32, (L, L), 1)
    tri_prev = jnp.where(kj >= qi, 0.0, -jnp.inf)
    tri_next = jnp.where(kj <= qi, 0.0, -jnp.inf)
    edge_prev = jnp.where((kj >= qi) & (t_in_seq > 0), 0.0, -jnp.inf)
    edge_next = jnp.where((kj <= qi) & (t_in_seq < tiles_per_seq - 1), 0.0, -jnp.inf)
    bias_prev = [edge_prev] + [tri_prev] * (n_blk - 1)
    bias_next = [tri_next] * (n_blk - 1) + [edge_next]
    sinks2 = sink_ref[...] * LOG2E

    def mlstm_branch():
        hsum = hf_ref[...] + hb_ref[...]
        parts = []
        for h in range(M_HEADS):
            hs = slice(h * M_HEAD_DIM, (h + 1) * M_HEAD_DIM)
            hh = hsum[hs]
            hh = hh - jnp.mean(hh, axis=0, keepdims=True)
            hh = hh * lax.rsqrt(jnp.mean(hh * hh, axis=0, keepdims=True) + EPS)
            w = jnp.concatenate([mhw_ref[hs, :]] * (T // LANES), axis=1)
            parts.append((_sigmoid(mo_ref[hs, :].astype(F32)) * (hh * w)).astype(BF16))
        hg_t = jnp.concatenate(parts, axis=0)
        return lax.dot_general(hg_t, wpm_ref[...], (((0,), (0,)), ((), ())), preferred_element_type=F32)

    o_cols = []
    s_next = scores(1)
    gate_m = jnp.dot(xb, wg_ref[:, :D_MODEL], preferred_element_type=F32)
    o_cols.append(attend(0, s))
    y_m = mlstm_branch()
    s, s_next = s_next, scores(2)
    tail(slice(0, T // 2))
    o_cols.append(attend(1, s))
    s, s_next = s_next, scores(3)
    gate_a = jnp.dot(xb, wg_ref[:, D_MODEL:], preferred_element_type=F32)
    o_cols.append(attend(2, s))
    tail(slice(T // 2, T))
    o_cols.append(attend(3, s_next))
    y_m = _sigmoid(gate_m) * y_m

    oa_s[...] = jnp.concatenate(o_cols, axis=1)
    ym_s[...] = y_m
    sa_s[...] = _sigmoid(gate_a)


def _mixer_out(x2, xb, seq_len, hf, hb, t_feat, mhw, aqkv, sink, wg, wpm, wpa, wo):
    n_tok = x2.shape[0]
    T, L = TOKEN_TILE, A_BLOCK
    n_tiles = n_tok // T
    tiles_per_seq = seq_len // T
    r = T // L
    n_lblk = n_tok // L
    cur = lambda i: jnp.minimum(i, n_tiles - 1)
    lag = lambda i: jnp.maximum(i - 1, 0)
    full = lambda a: _resident(a.shape)
    tile = lambda width, c: pl.BlockSpec((T, width), lambda i: (cur(i), c))
    feat = lambda c: pl.BlockSpec((M_WIDTH, T), lambda i: (c, cur(i)))
    prev = lambda c: pl.BlockSpec((L, A_WIDTH), lambda i: (jnp.maximum(cur(i) * r - 1, 0), c))
    nxt = lambda c: pl.BlockSpec((L, A_WIDTH), lambda i: (jnp.minimum((cur(i) + 1) * r, n_lblk - 1), c))
    lagged = pl.BlockSpec((T, D_MODEL), lambda i: (lag(i), 0))
    return pl.pallas_call(
        functools.partial(_mixer_out_kernel, n_tiles=n_tiles, tiles_per_seq=tiles_per_seq),
        grid=(n_tiles + 1,),
        in_specs=[
            tile(D_MODEL, 0), lagged,
            feat(0), feat(0), feat(2), full(mhw),
            tile(A_WIDTH, 0),
            prev(1), tile(A_WIDTH, 1), nxt(1),
            prev(2), tile(A_WIDTH, 2), nxt(2),
            full(sink), full(wg), full(wpm), full(wpa), full(wo),
        ],
        out_specs=lagged,
        out_shape=jax.ShapeDtypeStruct((n_tok, D_MODEL), F32),
        scratch_shapes=[pltpu.VMEM((T, A_WIDTH), BF16), pltpu.VMEM((T, D_MODEL), F32),
                        pltpu.VMEM((T, D_MODEL), F32)],
        compiler_params=pltpu.CompilerParams(
            dimension_semantics=("arbitrary",), vmem_limit_bytes=VMEM_LIMIT),
        name="mixer_out",
    )(xb, x2, hf, hb, t_feat, mhw, aqkv, aqkv, aqkv, aqkv, aqkv, aqkv, aqkv,
      sink, wg, wpm, wpa, wo)


def _conv_ffn_kernel(hp_ref, hc_ref, hn_ref, nw_ref, wua_ref, wub_ref, cw_ref, cb_ref, wd_ref,
                     nfw_ref, out_ref, xn_ref, z_ref, *, tiles_per_seq):
    T, HALO, C = FFN_TILE, FFN_HALO, FFN_CHUNK
    t_in_seq = pl.program_id(0) % tiles_per_seq
    nw = nw_ref[...]
    x = hc_ref[...]
    keep_prev = jnp.where(t_in_seq == 0, 0.0, 1.0)
    keep_next = jnp.where(t_in_seq == tiles_per_seq - 1, 0.0, 1.0)
    xn_ref[0:HALO, :] = (_rmsnorm(hp_ref[...], nw) * keep_prev).astype(BF16)
    xn_ref[HALO:HALO + T, :] = _rmsnorm(x, nw).astype(BF16)
    xn_ref[HALO + T:, :] = (_rmsnorm(hn_ref[...], nw) * keep_next).astype(BF16)
    rows = T + 2 * HALO

    def conv(u, cw, cb):
        up = pltpu.roll(u, 1, 0)[HALO:HALO + T]
        un = pltpu.roll(u, rows - 1, 0)[HALO:HALO + T]
        return up * cw[0:1] + u[HALO:HALO + T] * cw[1:2] + un * cw[2:3] + cb

    def up(c):
        cs = slice(c * C, (c + 1) * C)
        return (jnp.dot(xn_ref[...], wua_ref[:, cs], preferred_element_type=F32),
                jnp.dot(xn_ref[...], wub_ref[:, cs], preferred_element_type=F32))

    def act(c, ua, ub):
        cs = slice(c * C, (c + 1) * C)
        gs = slice(D_FF + c * C, D_FF + (c + 1) * C)
        a = conv(ua, cw_ref[:, cs], cb_ref[:, cs])
        b = conv(ub, cw_ref[:, gs], cb_ref[:, gs])
        return (a * _sigmoid(a) * b).astype(BF16)

    n_chunks = D_FF // C
    u = up(0)
    for c in range(n_chunks):
        u_next = up(c + 1) if c + 1 < n_chunks else None
        z_ref[:, c * C:(c + 1) * C] = act(c, *u)
        u = u_next
    halves = (slice(0, T // 2), slice(T // 2, T))
    ffn = [jnp.dot(z_ref[rs, :], wd_ref[...], preferred_element_type=F32) for rs in halves]
    for rs, f in zip(halves, ffn):
        out_ref[rs, :] = _rmsnorm(x[rs] + f, nfw_ref[...])


def _conv_ffn(h1, seq_len, nw, wua, wub, cw, cb, wd, nfw):
    n_tok = h1.shape[0]
    T, HALO = FFN_TILE, FFN_HALO
    tiles_per_seq = seq_len // T
    r = T // HALO
    n_hblk = n_tok // HALO
    full = lambda a: _resident(a.shape)
    return pl.pallas_call(
        functools.partial(_conv_ffn_kernel, tiles_per_seq=tiles_per_seq),
        grid=(n_tok // T,),
        in_specs=[
            pl.BlockSpec((HALO, D_MODEL), lambda i: (jnp.maximum(i * r - 1, 0), 0)),
            pl.BlockSpec((T, D_MODEL), lambda i: (i, 0)),
            pl.BlockSpec((HALO, D_MODEL), lambda i: (jnp.minimum((i + 1) * r, n_hblk - 1), 0)),
            full(nw), full(wua), full(wub), full(cw), full(cb), full(wd), full(nfw),
        ],
        out_specs=pl.BlockSpec((T, D_MODEL), lambda i: (i, 0)),
        out_shape=jax.ShapeDtypeStruct((n_tok, D_MODEL), F32),
        scratch_shapes=[pltpu.VMEM((T + 2 * HALO, D_MODEL), BF16), pltpu.VMEM((T, D_FF), BF16)],
        compiler_params=pltpu.CompilerParams(
            dimension_semantics=("arbitrary",), vmem_limit_bytes=VMEM_LIMIT),
        name="conv_ffn",
    )(h1, h1, h1, nw, wua, wub, cw, cb, wd, nfw)


def _rope_table(seq_len):
    half = ROT_DIM // 2
    inv = ROPE_THETA ** (-jnp.arange(half, dtype=F32) / half)
    ang = jnp.arange(seq_len, dtype=F32)[:, None] * inv[None, :]
    cs = jnp.concatenate([jnp.cos(ang), jnp.sin(ang)], axis=1)
    return jnp.pad(cs, ((0, 0), (0, LANES - ROT_DIM)))


def _encoder(x, p, rope):
    n_seq, seq_len, _ = x.shape
    assert seq_len % max(IN_TILE, TOKEN_TILE, FFN_TILE, MLSTM_GROUP * M_CHUNK) == 0, seq_len
    assert (n_seq * seq_len) % (SCAN_CHUNKS * M_CHUNK) == 0, (n_seq, seq_len)
    x2 = x.reshape(n_seq * seq_len, D_MODEL)
    k_tok, t_feat, gate_rows, aqkv, xb = _in_proj(x2, seq_len, p["norm1_w"], p["w_mk"], p["w_mt"],
                                                  p["w_a"], p["gate_bias"], rope)
    gs, uc = _gate_scan(gate_rows)
    hf, hb = _mlstm(k_tok, t_feat, gs, uc, n_seq, seq_len)
    h1 = _mixer_out(x2, xb, seq_len, hf, hb, t_feat, p["mh_norm_w"], aqkv, p["sink"],
                    p["w_branch_gate"], p["w_proj_m"], p["w_proj_a"], p["w_out"])
    y = _conv_ffn(h1, seq_len, p["norm2_w"], p["w_up_a"], p["w_up_b"], p["conv_w"], p["conv_b"],
                  p["w_down"], p["norm_f_w"])
    return y.reshape(n_seq, seq_len, D_MODEL)


def _prepare_params(norm1_w, w_in, i_bias, f_bias, mh_norm_w, attn_sink, w_proj_m, w_proj_a,
                    w_out, norm2_w, w_up, conv_w, conv_b, w_down, norm_f_w):
    w_in = w_in[0]
    W = M_WIDTH
    m_end = 4 * W
    g_end = m_end + 4 * M_HEADS
    a_end = g_end + A_WIDTH + 2 * A_KV_WIDTH
    w_mt = jnp.concatenate([w_in[:, 0:W], w_in[:, 2 * W:4 * W], w_in[:, m_end:g_end]],
                           axis=1).astype(BF16).T
    gate_bias = jnp.concatenate([i_bias[0].reshape(-1), f_bias[0].reshape(-1)])
    return {
        "norm1_w": norm1_w[0].reshape(1, D_MODEL),
        "w_mk": w_in[:, W:2 * W].astype(BF16),
        "w_mt": w_mt,
        "gate_bias": jnp.broadcast_to(gate_bias[:, None], (4 * M_HEADS, LANES)),
        "w_a": w_in[:, g_end:a_end].astype(BF16),
        "w_branch_gate": w_in[:, a_end:].astype(BF16),
        "mh_norm_w": jnp.broadcast_to(mh_norm_w[0][:, None], (M_WIDTH, LANES)),
        "sink": jnp.broadcast_to(attn_sink[0][:, None], (A_HEADS, LANES)),
        "w_proj_m": w_proj_m[0].astype(BF16),
        "w_proj_a": w_proj_a[0].astype(BF16),
        "w_out": w_out[0].astype(BF16),
        "norm2_w": norm2_w[0].reshape(1, D_MODEL),
        "w_up_a": w_up[0][:, :D_FF].astype(BF16),
        "w_up_b": w_up[0][:, D_FF:].astype(BF16),
        "conv_w": conv_w[0],
        "conv_b": conv_b[0].reshape(1, 2 * D_FF),
        "w_down": w_down[0].astype(BF16),
        "norm_f_w": norm_f_w.reshape(1, D_MODEL),
    }


def kernel(x_prompt, x_sample, norm1_w, w_in, i_bias, f_bias, mh_norm_w, attn_sink, w_proj_m, w_proj_a,
           w_out, norm2_w, w_up, conv_w, conv_b, w_down, norm_f_w):
    p = _prepare_params(norm1_w, w_in, i_bias, f_bias, mh_norm_w, attn_sink, w_proj_m, w_proj_a,
                        w_out, norm2_w, w_up, conv_w, conv_b, w_down, norm_f_w)
    rope = _rope_table(max(x_prompt.shape[1], x_sample.shape[1]))
    return (_encoder(x_prompt, p, rope), _encoder(x_sample, p, rope))
```
